```python
import jax, jax.numpy as jnp
from jax import lax
import numpy as np

D_MODEL = 1024
BATCH = 16
SEQ = 2048
DEPTH = 4

CTX_LEN = 256
GRID_W = 64
HEAD_DIM = 64
ATTN_Q_HEADS = D_MODEL // 128
ATTN_KV_HEADS = ATTN_Q_HEADS // 4
ATTN_GROUP = ATTN_Q_HEADS // ATTN_KV_HEADS
MLSTM_HEADS = D_MODEL // 256
MLSTM_DIM = 64
CMLP_GROUPS = D_MODEL // 256
CMLP_DIM = 64
CMLP_CHUNK = 128
Q_BLOCK = 128
MLSTM_CHUNK = 64
D_FF = 2816
CONV_W = 3
ROPE_THETA = 10000.0
EPS = 1e-6
ATTN_W = ATTN_Q_HEADS * HEAD_DIM
KV_W = ATTN_KV_HEADS * HEAD_DIM
MLSTM_W = MLSTM_HEADS * MLSTM_DIM
CMLP_W = CMLP_GROUPS * CMLP_DIM
D_MIX = ATTN_W + MLSTM_W + CMLP_W
N_GATES = 4 * MLSTM_HEADS
IN_SPLITS = (ATTN_W, KV_W, KV_W, MLSTM_W, MLSTM_W, MLSTM_W, MLSTM_W, N_GATES, CMLP_W, CMLP_W)
D_IN = ATTN_W + 2 * KV_W + 4 * MLSTM_W + N_GATES + 2 * CMLP_W
GATE_OFF = ATTN_W + 2 * KV_W + 4 * MLSTM_W

kernel_name = "hybrid_attn_mlstm_gmlp_dit_block"


def rmsnorm(x, g):
    xf = x.astype(jnp.float32)
    y = xf * lax.rsqrt(jnp.mean(xf * xf, axis=-1, keepdims=True) + EPS)
    return (y * g.astype(jnp.float32)).astype(x.dtype)


def axial_rope_tables(n):
    rows = n // GRID_W
    t = jnp.arange(n)
    row = jnp.repeat(jnp.arange(rows), GRID_W).astype(jnp.float32)
    col = (t % GRID_W).astype(jnp.float32)
    nf = HEAD_DIM // 4
    inv = ROPE_THETA ** (-jnp.arange(nf, dtype=jnp.float32) / nf)
    ang = jnp.concatenate([row[:, None] * inv[None], col[:, None] * inv[None]], axis=-1)
    return jnp.cos(ang), jnp.sin(ang)


def apply_rope(x, cos, sin):
    n = x.shape[1]
    nf = HEAD_DIM // 4
    xf = x.astype(jnp.float32)
    xr = xf.reshape(*xf.shape[:-1], 2, 2, nf)
    x1, x2 = xr[..., 0, :], xr[..., 1, :]
    c = cos.reshape(n, 2, nf)[None, :, None]
    s = sin.reshape(n, 2, nf)[None, :, None]
    out = jnp.stack([x1 * c - x2 * s, x2 * c + x1 * s], axis=-2)
    return out.reshape(xf.shape).astype(x.dtype)


def gqa_attend(q, k, v):
    s = jnp.einsum('bqkgd,bskd->bkgqs', q, k, preferred_element_type=jnp.float32) * (HEAD_DIM ** -0.5)
    p = jax.nn.softmax(s, axis=-1).astype(v.dtype)
    return jnp.einsum('bkgqs,bskd->bqkgd', p, v)


def latent_attention(q, k_all, v_all):
    B, S = q.shape[0], q.shape[1]
    nb = S // Q_BLOCK
    qb = jnp.moveaxis(q.reshape(B, nb, Q_BLOCK, *q.shape[2:]), 1, 0)
    ob = lax.map(lambda blk: gqa_attend(blk, k_all, v_all), qb)
    return jnp.moveaxis(ob, 0, 1).reshape(B, S, -1)


def mlstm_scan(q, k, v, ig, fg, state):
    B, H, N, d = q.shape
    L = MLSTM_CHUNK
    nc = N // L
    chunk = lambda t: jnp.moveaxis(t.reshape(B, H, nc, L, *t.shape[3:]), 2, 0)
    tril = jnp.tril(jnp.ones((L, L), dtype=bool))

    def body(carry, inp):
        C, n, m = carry
        qc, kc, vc, ic, fc = inp
        b = jnp.cumsum(jax.nn.log_sigmoid(fc), axis=-1)
        Dm = jnp.where(tril, b[..., :, None] - b[..., None, :] + ic[..., None, :], -jnp.inf)
        m_inter = b + m[..., None]
        m_t = jnp.maximum(m_inter, jnp.max(Dm, axis=-1))
        w_inter = jnp.exp(m_inter - m_t)
        A = jnp.exp(Dm - m_t[..., None]) * jnp.einsum('bhtd,bhsd->bhts', qc, kc)
        num = w_inter[..., None] * jnp.einsum('bhtd,bhde->bhte', qc, C) + jnp.einsum('bhts,bhse->bhte', A, vc)
        den = w_inter * jnp.einsum('bhtd,bhd->bht', qc, n) + jnp.sum(A, axis=-1)
        h = num / jnp.maximum(jnp.abs(den), jnp.exp(-m_t))[..., None]
        m_new = m_t[..., -1]
        g_s = jnp.exp(b[..., -1:] - b + ic - m_new[..., None])
        w_c = jnp.exp(b[..., -1] + m - m_new)
        C_new = w_c[..., None, None] * C + jnp.einsum('bhs,bhsd,bhse->bhde', g_s, kc, vc)
        n_new = w_c[..., None] * n + jnp.einsum('bhs,bhsd->bhd', g_s, kc)
        return (C_new, n_new, m_new), h

    state_out, hs = lax.scan(body, state, (chunk(q), chunk(k), chunk(v), chunk(ig), chunk(fg)))
    return jnp.moveaxis(hs, 0, 2).reshape(B, H, N, d), state_out


def mlstm_dir(q, k, v, ig, fg, state, reverse):
    if reverse:
        h, st = mlstm_scan(jnp.flip(q, 2), jnp.flip(k, 2), jnp.flip(v, 2),
                           jnp.flip(ig, -1), jnp.flip(fg, -1), state)
        return jnp.flip(h, 2), st
    return mlstm_scan(q, k, v, ig, fg, state)


def mlstm_merge(h_f, h_b, o, g_mh):
    B, H, N, d = h_f.shape
    hs = jnp.transpose(h_f + h_b, (0, 2, 1, 3))
    hs = rmsnorm(hs, g_mh.reshape(H, d)).reshape(B, N, H * d)
    return (jax.nn.sigmoid(o.astype(jnp.float32)) * hs).astype(o.dtype)


def chunk_mlp(u, v, g_v, w_sp, b_sp):
    B, N, _ = u.shape
    nc = N // CMLP_CHUNK
    u = jax.nn.gelu(u)
    v = rmsnorm(jax.nn.gelu(v).reshape(B, N, CMLP_GROUPS, CMLP_DIM), g_v.reshape(CMLP_GROUPS, CMLP_DIM))
    vb = v.reshape(B, nc, CMLP_CHUNK, CMLP_GROUPS, CMLP_DIM)
    z = jnp.einsum('gpq,bcqgd->bcpgd', w_sp, vb) + jnp.transpose(b_sp)[None, None, :, :, None]
    return u * z.reshape(B, N, CMLP_W)


def conv_ffn(h, w_up, conv_w, conv_b, w_down):
    a = h @ w_up
    n = a.shape[1]
    ap = jnp.pad(a, ((0, 0), (1, 1), (0, 0)))
    a = ap[:, 0:n] * conv_w[0] + ap[:, 1:n + 1] * conv_w[1] + ap[:, 2:n + 2] * conv_w[2] + conv_b
    gate, val = jnp.split(a, 2, axis=-1)
    return (jax.nn.silu(gate) * val) @ w_down


def token_mixers(h, hc, w_in, b_in, g_q, g_k, g_mh, g_v, w_sp, b_sp, need_ctx):
    B, S, _ = h.shape
    T = hc.shape[1]
    split_at = [int(s) for s in np.cumsum(IN_SPLITS)[:-1]]
    qa, ka, va, qm, km, vm, om, gt, uc, vc = jnp.split(h @ w_in + b_in, split_at, axis=-1)
    qa_c, ka_c, va_c, qm_c, km_c, vm_c, om_c, gt_c, uc_c, vc_c = jnp.split(hc @ w_in + b_in, split_at, axis=-1)

    cos, sin = axial_rope_tables(S)
    q_l = apply_rope(rmsnorm(qa.reshape(B, S, ATTN_Q_HEADS, HEAD_DIM), g_q), cos, sin)
    k_l = apply_rope(rmsnorm(ka.reshape(B, S, ATTN_KV_HEADS, HEAD_DIM), g_k), cos, sin)
    k_c = rmsnorm(ka_c.reshape(B, T, ATTN_KV_HEADS, HEAD_DIM), g_k)
    v_l = va.reshape(B, S, ATTN_KV_HEADS, HEAD_DIM)
    v_c = va_c.reshape(B, T, ATTN_KV_HEADS, HEAD_DIM)
    k_all = jnp.concatenate([k_c, k_l], axis=1)
    v_all = jnp.concatenate([v_c, v_l], axis=1)
    attn_l = latent_attention(q_l.reshape(B, S, ATTN_KV_HEADS, ATTN_GROUP, HEAD_DIM), k_all, v_all)

    def heads(t, n):
        return jnp.transpose(t.reshape(B, n, MLSTM_HEADS, MLSTM_DIM), (0, 2, 1, 3)).astype(jnp.float32)

    def gates(t, n):
        return jnp.transpose(t.reshape(B, n, 4, MLSTM_HEADS), (2, 0, 3, 1)).astype(jnp.float32)

    kscale = MLSTM_DIM ** -0.5
    q_mc, k_mc, v_mc, g_c = heads(qm_c, T), heads(km_c, T) * kscale, heads(vm_c, T), gates(gt_c, T)
    q_ml, k_ml, v_ml, g_l = heads(qm, S), heads(km, S) * kscale, heads(vm, S), gates(gt, S)
    zero = (jnp.zeros((B, MLSTM_HEADS, MLSTM_DIM, MLSTM_DIM), jnp.float32),
            jnp.zeros((B, MLSTM_HEADS, MLSTM_DIM), jnp.float32),
            jnp.zeros((B, MLSTM_HEADS), jnp.float32))
    hf_c, st_f = mlstm_dir(q_mc, k_mc, v_mc, g_c[0], g_c[1], zero, False)
    hb_c, st_b = mlstm_dir(q_mc, k_mc, v_mc, g_c[2], g_c[3], zero, True)
    hf_l, _ = mlstm_dir(q_ml, k_ml, v_ml, g_l[0], g_l[1], st_f, False)
    hb_l, _ = mlstm_dir(q_ml, k_ml, v_ml, g_l[2], g_l[3], st_b, True)
    mlstm_l = mlstm_merge(hf_l, hb_l, om, g_mh)

    cmlp_l = chunk_mlp(uc, vc, g_v, w_sp, b_sp)

    mix_l = jnp.concatenate([attn_l, mlstm_l, cmlp_l], axis=-1)
    if not need_ctx:
        return mix_l, None
    q_c = rmsnorm(qa_c.reshape(B, T, ATTN_Q_HEADS, HEAD_DIM), g_q)
    attn_c = gqa_attend(q_c.reshape(B, T, ATTN_KV_HEADS, ATTN_GROUP, HEAD_DIM), k_c, v_c).reshape(B, T, ATTN_W)
    mlstm_c = mlstm_merge(hf_c, hb_c, om_c, g_mh)
    cmlp_c = chunk_mlp(uc_c, vc_c, g_v, w_sp, b_sp)
    mix_c = jnp.concatenate([attn_c, mlstm_c, cmlp_c], axis=-1)
    return mix_l, mix_c


def setup_inputs(seed: int = 0) -> dict:
    key = jax.random.key(seed)
    ks = jax.random.split(key, 24)
    nrm = lambda k, shape, s: jax.random.normal(k, shape, jnp.float32) * s
    forget_off = jnp.zeros((D_IN,), jnp.float32)
    fbias = jnp.linspace(3.0, 6.0, MLSTM_HEADS)
    forget_off = forget_off.at[GATE_OFF + MLSTM_HEADS:GATE_OFF + 2 * MLSTM_HEADS].set(fbias)
    forget_off = forget_off.at[GATE_OFF + 3 * MLSTM_HEADS:GATE_OFF + 4 * MLSTM_HEADS].set(fbias)
    return {
        "x": nrm(ks[0], (BATCH, SEQ, D_MODEL), 1.0),
        "c": nrm(ks[1], (BATCH, D_MODEL), 1.0),
        "ctx": nrm(ks[2], (BATCH, CTX_LEN, D_MODEL), 1.0),
        "c_ctx": nrm(ks[3], (D_MODEL,), 1.0),
        "w_ada": nrm(ks[4], (DEPTH, D_MODEL, 6 * D_MODEL), 0.5 * D_MODEL ** -0.5),
        "b_ada": nrm(ks[5], (DEPTH, 6 * D_MODEL), 0.02),
        "g_norm1": 1.0 + nrm(ks[6], (DEPTH, D_MODEL), 0.02),
        "w_in": nrm(ks[7], (DEPTH, D_MODEL, D_IN), D_MODEL ** -0.5),
        "b_in": nrm(ks[8], (DEPTH, D_IN), 0.02) + forget_off[None],
        "g_q": 1.0 + nrm(ks[9], (DEPTH, HEAD_DIM), 0.02),
        "g_k": 1.0 + nrm(ks[10], (DEPTH, HEAD_DIM), 0.02),
        "g_mh": 1.0 + nrm(ks[11], (DEPTH, MLSTM_W), 0.02),
        "g_v": 1.0 + nrm(ks[12], (DEPTH, CMLP_W), 0.02),
        "w_sp": nrm(ks[13], (DEPTH, CMLP_GROUPS, CMLP_CHUNK, CMLP_CHUNK), 0.5 * CMLP_CHUNK ** -0.5),
        "b_sp": 1.0 + nrm(ks[14], (DEPTH, CMLP_GROUPS, CMLP_CHUNK), 0.02),
        "w_out": nrm(ks[15], (DEPTH, D_MIX, D_MODEL), D_MIX ** -0.5),
        "g_norm2": 1.0 + nrm(ks[16], (DEPTH, D_MODEL), 0.02),
        "w_up": nrm(ks[17], (DEPTH, D_MODEL, 2 * D_FF), D_MODEL ** -0.5),
        "conv_w": nrm(ks[18], (DEPTH, CONV_W, 2 * D_FF), CONV_W ** -0.5),
        "conv_b": nrm(ks[19], (DEPTH, 2 * D_FF), 0.02),
        "w_down": nrm(ks[20], (DEPTH, D_FF, D_MODEL), D_FF ** -0.5),
    }


def reference(x, c, ctx, c_ctx, w_ada, b_ada, g_norm1, w_in, b_in, g_q, g_k, g_mh, g_v,
              w_sp, b_sp, w_out, g_norm2, w_up, conv_w, conv_b, w_down):
    sc = jax.nn.silu(c)
    scc = jax.nn.silu(c_ctx)
    xc = ctx
    for l in range(DEPTH):
        last = l == DEPTH - 1
        mod = (sc @ w_ada[l] + b_ada[l])[:, None, :]
        mod_c = (scc @ w_ada[l] + b_ada[l])[None, None, :]
        sh1, s1, g1, sh2, s2, g2 = jnp.split(mod, 6, axis=-1)
        sh1c, s1c, g1c, sh2c, s2c, g2c = jnp.split(mod_c, 6, axis=-1)
        h = rmsnorm(x, g_norm1[l]) * (1 + s1) + sh1
        hc = rmsnorm(xc, g_norm1[l]) * (1 + s1c) + sh1c
        mix, mix_c = token_mixers(h, hc, w_in[l], b_in[l], g_q[l], g_k[l], g_mh[l], g_v[l],
                                  w_sp[l], b_sp[l], not last)
        x = x + g1 * (mix @ w_out[l])
        h2 = rmsnorm(x, g_norm2[l]) * (1 + s2) + sh2
        x = x + g2 * conv_ffn(h2, w_up[l], conv_w[l], conv_b[l], w_down[l])
        if not last:
            xc = xc + g1c * (mix_c @ w_out[l])
            h2c = rmsnorm(xc, g_norm2[l]) * (1 + s2c) + sh2c
            xc = xc + g2c * conv_ffn(h2c, w_up[l], conv_w[l], conv_b[l], w_down[l])
    return x
```

```python
import functools

import jax
import jax.numpy as jnp
import numpy as np
from jax import lax
from jax.experimental import pallas as pl
from jax.experimental.pallas import tpu as pltpu

F32 = jnp.float32
BF16 = jnp.bfloat16

D_MODEL = 1024
HEAD_DIM = 64
N_Q_HEADS = 8
N_KV_HEADS = 2
GRID_W = 64
ROPE_THETA = 10000.0
EPS = 1e-6
ATTN_W = N_Q_HEADS * HEAD_DIM
KV_W = N_KV_HEADS * HEAD_DIM
ML_HEADS = 4
ML_W = ML_HEADS * HEAD_DIM
CM_GROUPS = 4
CM_W = CM_GROUPS * HEAD_DIM
CM_CHUNK = 128
N_GATES = 4 * ML_HEADS
D_FF = 2816
LANES = 128

OFF_Q, OFF_K, OFF_V = 0, 512, 640
OFF_MQ, OFF_MO, OFF_CU, OFF_GT = 768, 1536, 1792, 2304
N_PROJ = 2432
QK_W = ATTN_W + KV_W

ROW_TILE = 256
ML_CHUNK = 256
FF_TILE = 256
HALO = 16
VMEM_LIMIT = 56 * 1024 * 1024

Q_HEAD_ORDER = (0, 4, 1, 5, 2, 6, 3, 7)


def _cparams(*sem):
    return pltpu.CompilerParams(dimension_semantics=sem, vmem_limit_bytes=VMEM_LIMIT)


def _group_mean_sq(x, g_ones):
    x2 = x * x
    hi = x2.astype(BF16)
    lo = (x2 - hi.astype(F32)).astype(BF16)
    cols = []
    for j in range(x.shape[1] // LANES):
        sl = slice(j * LANES, (j + 1) * LANES)
        cols.append(jnp.dot(hi[:, sl], g_ones, preferred_element_type=F32)
                    + jnp.dot(lo[:, sl], g_ones, preferred_element_type=F32))
    s = cols[0] if len(cols) == 1 else jnp.concatenate(cols, axis=1)
    return s * (1.0 / HEAD_DIM)


def _mod_kernel(c_ref, w_ref, b_ref, o_ref):
    c = c_ref[...]
    sc = c * jax.nn.sigmoid(c)
    o_ref[0] = jnp.dot(sc, w_ref[0], preferred_element_type=F32,
                       precision=lax.Precision.HIGHEST) + b_ref[0]


def _modulation(cs, w_ada, b_ada):
    depth, d, d6 = w_ada.shape
    r = cs.shape[0]
    nj = d6 // d
    return pl.pallas_call(
        _mod_kernel,
        grid=(depth, nj),
        in_specs=[pl.BlockSpec((r, d), lambda l, j: (0, 0)),
                  pl.BlockSpec((1, d, d), lambda l, j: (l, 0, j)),
                  pl.BlockSpec((1, 1, d), lambda l, j: (l, 0, j))],
        out_specs=pl.BlockSpec((1, r, d), lambda l, j: (l, 0, j)),
        out_shape=jax.ShapeDtypeStruct((depth, r, d6), F32),
        compiler_params=_cparams("arbitrary", "arbitrary"),
        name="adaln_mod",
    )(cs, w_ada, b_ada.reshape(depth, 1, d6))


def _in_kernel(x_ref, mod_ref, g1_ref, w_ref, b_ref, gqk_ref, gones_ref, cos_ref, sa_ref, sb_ref,
               qk_ref, v_ref, mqkv_ref, mo_ref, cuv_ref, gt_ref):
    d = D_MODEL
    x = x_ref[0]
    mod = mod_ref[0]
    sh1, s1 = mod[:, 0:d], mod[:, d:2 * d]
    ms = jnp.mean(x * x, axis=-1, keepdims=True)
    h = (x * lax.rsqrt(ms + EPS)) * g1_ref[...] * (1 + s1) + sh1
    p = jnp.dot(h.astype(BF16), w_ref[...], preferred_element_type=F32) + b_ref[...]

    qk = p[:, OFF_Q:OFF_Q + QK_W]
    gms = _group_mean_sq(qk, gones_ref[...])
    qn = (qk * lax.rsqrt(gms + EPS)) * gqk_ref[...]
    cos, sa, sb = cos_ref[...], sa_ref[...], sb_ref[...]
    for j in range(QK_W // LANES):
        xb = qn[:, j * LANES:(j + 1) * LANES]
        up = pltpu.roll(xb, LANES - 16, axis=1)
        dn = pltpu.roll(xb, 16, axis=1)
        qk_ref[0, :, j * LANES:(j + 1) * LANES] = (xb * cos + up * sa + dn * sb).astype(BF16)

    v_ref[0] = p[:, OFF_V:OFF_V + KV_W].astype(BF16)
    mqkv_ref[0, :, 0:ML_W] = p[:, OFF_MQ:OFF_MQ + ML_W].astype(BF16)
    mqkv_ref[0, :, ML_W:2 * ML_W] = (p[:, OFF_MQ + ML_W:OFF_MQ + 2 * ML_W] * (HEAD_DIM ** -0.5)).astype(BF16)
    mqkv_ref[0, :, 2 * ML_W:3 * ML_W] = p[:, OFF_MQ + 2 * ML_W:OFF_MQ + 3 * ML_W].astype(BF16)
    mo_ref[0] = p[:, OFF_MO:OFF_MO + ML_W]
    cuv_ref[0] = p[:, OFF_CU:OFF_CU + 2 * CM_W]
    gt_ref[0] = p[:, OFF_GT:OFF_GT + LANES]


def _in_proj(x_all, mod_l, g1, w_in, b_in, gqk, gones, cos_t, sa_t, sb_t, n_ctx_tiles):
    bsz, tt, d = x_all.shape
    tm = ROW_TILE
    nt = tt // tm
    row = lambda b, t: (b, t, 0)
    const = lambda b, t: (0, 0)
    tab = lambda b, t: (t, 0)
    outs = [(QK_W, BF16), (KV_W, BF16), (3 * ML_W, BF16), (ML_W, F32), (2 * CM_W, F32), (LANES, F32)]
    return pl.pallas_call(
        _in_kernel,
        grid=(bsz, nt),
        in_specs=[pl.BlockSpec((1, tm, d), row),
                  pl.BlockSpec((1, 1, 6 * d), lambda b, t: (jnp.where(t < n_ctx_tiles, bsz, b), 0, 0)),
                  pl.BlockSpec((1, d), const),
                  pl.BlockSpec((d, N_PROJ), const),
                  pl.BlockSpec((1, N_PROJ), const),
                  pl.BlockSpec((1, QK_W), const),
                  pl.BlockSpec((LANES, LANES), const),
                  pl.BlockSpec((tm, LANES), tab),
                  pl.BlockSpec((tm, LANES), tab),
                  pl.BlockSpec((tm, LANES), tab)],
        out_specs=[pl.BlockSpec((1, tm, w), row) for w, _ in outs],
        out_shape=[jax.ShapeDtypeStruct((bsz, tt, w), dt) for w, dt in outs],
        compiler_params=_cparams("parallel", "parallel"),
        name="in_proj",
    )(x_all, mod_l, g1, w_in, b_in, gqk, gones, cos_t, sa_t, sb_t)


def _attn_kernel(q_ref, k_ref, v_ref, o_ref, *, n_ctx_tiles, t_ctx):
    t = pl.program_id(1)
    tq = q_ref.shape[1]
    lane = lax.broadcasted_iota(jnp.int32, (tq, LANES), 1)
    low = lane < HEAD_DIM

    def attend(kv_len):
        k = k_ref[0, 0:kv_len, :]
        v = v_ref[0, 0:kv_len, :]
        for pair in range(ATTN_W // LANES):
            qp = q_ref[0, :, pair * LANES:(pair + 1) * LANES]
            halves = []
            for sel in (low, jnp.logical_not(low)):
                qh = jnp.where(sel, qp, jnp.zeros_like(qp))
                s = lax.dot_general(qh, k, (((1,), (1,)), ((), ())), preferred_element_type=F32)
                m = jnp.max(s, axis=-1, keepdims=True)
                e = jnp.exp(s - m)
                l = jnp.sum(e, axis=-1, keepdims=True)
                o = jnp.dot(e.astype(BF16), v, preferred_element_type=F32)
                halves.append(o / l)
            o_ref[0, :, pair * LANES:(pair + 1) * LANES] = jnp.where(low, halves[0], halves[1]).astype(BF16)

    @pl.when(t < n_ctx_tiles)
    def _():
        attend(t_ctx)

    @pl.when(t >= n_ctx_tiles)
    def _():
        attend(k_ref.shape[1])


def _attention(qk, v, t_ctx):
    bsz, tt, _ = qk.shape
    tq = ROW_TILE
    kern = functools.partial(_attn_kernel, n_ctx_tiles=t_ctx // tq, t_ctx=t_ctx)
    return pl.pallas_call(
        kern,
        grid=(bsz, tt // tq),
        in_specs=[pl.BlockSpec((1, tq, ATTN_W), lambda b, t: (b, t, 0)),
                  pl.BlockSpec((1, tt, KV_W), lambda b, t: (b, 0, ATTN_W // KV_W)),
                  pl.BlockSpec((1, tt, KV_W), lambda b, t: (b, 0, 0))],
        out_specs=pl.BlockSpec((1, tq, ATTN_W), lambda b, t: (b, t, 0)),
        out_shape=jax.ShapeDtypeStruct((bsz, tt, ATTN_W), BF16),
        compiler_params=_cparams("parallel", "parallel"),
        name="attention",
    )(qk, qk, v)


def _log_sigmoid(x):
    return jnp.minimum(x, 0.0) - jnp.log1p(jnp.exp(-jnp.abs(x)))


def _split3(a):
    hi = a.astype(BF16)
    r1 = a - hi.astype(F32)
    mid = r1.astype(BF16)
    lo = (r1 - mid.astype(F32)).astype(BF16)
    return hi, mid, lo


def _mlstm_kernel(qkv_ref, g_ref, gt_ref, mo_ref, gmh_ref, gones_ref, tril_ref, triu_ref,
                  o_ref, hf_ref, hb_ref, cn_ref, m_ref, *, n_ctx_chunks):
    L = ML_CHUNK
    W = ML_W
    CNW = W + LANES
    n_chunks = qkv_ref.shape[1] // L

    row = lax.broadcasted_iota(jnp.int32, (L, L), 0)
    col = lax.broadcasted_iota(jnp.int32, (L, L), 1)
    lane_w = lax.broadcasted_iota(jnp.int32, (L, W), 1)
    lane_cn = lax.broadcasted_iota(jnp.int32, (1, CNW), 1)
    lane_m = lax.broadcasted_iota(jnp.int32, (1, LANES), 1)
    cn_row = lax.broadcasted_iota(jnp.int32, (W, CNW), 0) // HEAD_DIM
    cn_col = lax.broadcasted_iota(jnp.int32, (W, CNW), 1)
    cn_mask = jnp.where(cn_col < W, cn_col // HEAD_DIM, cn_col - W) == cn_row
    ones_blk = jnp.ones((L, LANES), BF16)

    cn_ref[...] = jnp.zeros_like(cn_ref)
    m_ref[...] = jnp.zeros_like(m_ref)

    def step(rev, c):
        r0 = pl.multiple_of(c * L, L)
        q = qkv_ref[0, pl.ds(r0, L), 0:W]
        k = qkv_ref[0, pl.ds(r0, L), W:2 * W]
        v = qkv_ref[0, pl.ds(r0, L), 2 * W:3 * W]
        g = g_ref[0, pl.ds(r0, L), :]
        gt = gt_ref[0, c]
        goff = 2 * ML_HEADS if rev else 0
        tri_c = triu_ref[...] if rev else tril_ref[...]
        tri_r = tril_ref[...] if rev else triu_ref[...]
        valid = (col >= row) if rev else (col <= row)
        last = 0 if rev else L - 1

        bcols = sum(jnp.dot(tri_c, part, preferred_element_type=F32)
                    for part in _split3(_log_sigmoid(g)))
        brows = sum(jnp.dot(part, tri_r, preferred_element_type=F32)
                    for part in _split3(_log_sigmoid(gt)))

        cn = cn_ref[rev]
        m_all = m_ref[rev]
        qc = jnp.dot(q, cn.astype(BF16), preferred_element_type=F32)

        h_out = jnp.zeros((L, W), F32)
        gs_all = jnp.zeros((L, W), F32)
        wc_all = jnp.zeros((1, CNW), F32)
        m_new_all = jnp.zeros((1, LANES), F32)
        for h in range(ML_HEADS):
            ic, fc = goff + h, goff + ML_HEADS + h
            bcol = bcols[:, fc:fc + 1]
            brow = brows[fc:fc + 1, :]
            irow = gt[ic:ic + 1, :]
            icol = g[:, ic:ic + 1]
            head = (lane_w // HEAD_DIM) == h
            dm = jnp.where(valid, bcol - brow + irow, -jnp.inf)
            m_prev = m_all[0:1, h:h + 1]
            m_inter = bcol + m_prev
            m_t = jnp.maximum(m_inter, jnp.max(dm, axis=-1, keepdims=True))
            w_inter = jnp.exp(m_inter - m_t)
            qh = jnp.where(head, q, jnp.zeros_like(q))
            s = lax.dot_general(qh, k, (((1,), (1,)), ((), ())), preferred_element_type=F32)
            a = jnp.exp(dm - m_t) * s
            a_sum = jnp.sum(a, axis=-1, keepdims=True)
            av = jnp.dot(a.astype(BF16), v, preferred_element_type=F32)
            num = w_inter * qc[:, 0:W] + av
            den = w_inter * qc[:, W + h:W + h + 1] + a_sum
            hh = num / jnp.maximum(jnp.abs(den), jnp.exp(-m_t))
            h_out = jnp.where(head, hh, h_out)

            b_last = bcol[last:last + 1, :]
            m_new = m_t[last:last + 1, :]
            gs = jnp.exp(b_last - bcol + icol - m_new)
            wc = jnp.exp(b_last + m_prev - m_new)
            gs_all = jnp.where(head, gs, gs_all)
            cn_head = jnp.where(lane_cn < W, lane_cn // HEAD_DIM, lane_cn - W) == h
            wc_all = jnp.where(cn_head, wc, wc_all)
            m_new_all = jnp.where(lane_m == h, m_new, m_new_all)

        gk = (k.astype(F32) * gs_all).astype(BF16)
        v1 = jnp.concatenate([v, ones_blk], axis=1)
        upd = lax.dot_general(gk, v1, (((0,), (0,)), ((), ())), preferred_element_type=F32)
        cn_ref[rev] = wc_all * cn + jnp.where(cn_mask, upd, 0.0)
        m_ref[rev] = jnp.broadcast_to(m_new_all, m_all.shape)
        dst = hb_ref if rev else hf_ref
        dst[pl.ds(r0, L), :] = h_out

    def body(j, carry):
        step(0, j)
        cb = jnp.where(j < n_ctx_chunks, n_ctx_chunks - 1 - j, n_chunks - 1 - (j - n_ctx_chunks))
        step(1, cb)
        return carry

    lax.fori_loop(0, n_chunks, body, 0)

    def merge(c, carry):
        r0 = pl.multiple_of(c * L, L)
        hs = hf_ref[pl.ds(r0, L), :] + hb_ref[pl.ds(r0, L), :]
        gms = _group_mean_sq(hs, gones_ref[...])
        y = (hs * lax.rsqrt(gms + EPS)) * gmh_ref[...]
        o_ref[0, pl.ds(r0, L), :] = (jax.nn.sigmoid(mo_ref[0, pl.ds(r0, L), :]) * y).astype(BF16)
        return carry

    lax.fori_loop(0, n_chunks, merge, 0)


def _mlstm(mqkv, gates, gates_t, mo, g_mh, gones, tril, triu, t_ctx):
    bsz, tt, _ = mqkv.shape
    L = ML_CHUNK
    nc = tt // L
    kern = functools.partial(_mlstm_kernel, n_ctx_chunks=t_ctx // L)
    per_b = lambda b: (b, 0, 0)
    const = lambda b: (0, 0)
    return pl.pallas_call(
        kern,
        grid=(bsz,),
        in_specs=[pl.BlockSpec((1, tt, 3 * ML_W), per_b),
                  pl.BlockSpec((1, tt, LANES), per_b),
                  pl.BlockSpec((1, nc, N_GATES, L), lambda b: (b, 0, 0, 0)),
                  pl.BlockSpec((1, tt, ML_W), per_b),
                  pl.BlockSpec((1, ML_W), const),
                  pl.BlockSpec((LANES, LANES), const),
                  pl.BlockSpec((L, L), const),
                  pl.BlockSpec((L, L), const)],
        out_specs=pl.BlockSpec((1, tt, ML_W), per_b),
        out_shape=jax.ShapeDtypeStruct((bsz, tt, ML_W), BF16),
        scratch_shapes=[pltpu.VMEM((tt, ML_W), F32), pltpu.VMEM((tt, ML_W), F32),
                        pltpu.VMEM((2, ML_W, ML_W + LANES), F32), pltpu.VMEM((2, 8, LANES), F32)],
        compiler_params=_cparams("parallel"),
        name="mlstm",
    )(mqkv, gates, gates_t, mo, g_mh, gones, tril, triu)


def _cmlp_kernel(uv_ref, gv_ref, gones_ref, wsp_ref, bsp_ref, o_ref):
    n_chunks = uv_ref.shape[1] // CM_CHUNK
    lane = lax.broadcasted_iota(jnp.int32, (CM_CHUNK, CM_W), 1)
    for c in range(n_chunks):
        rows = slice(c * CM_CHUNK, (c + 1) * CM_CHUNK)
        u = jax.nn.gelu(uv_ref[0, rows, 0:CM_W])
        v = jax.nn.gelu(uv_ref[0, rows, CM_W:2 * CM_W])
        gms = _group_mean_sq(v, gones_ref[...])
        vb = ((v * lax.rsqrt(gms + EPS)) * gv_ref[...]).astype(BF16)
        z = jnp.zeros((CM_CHUNK, CM_W), F32)
        for g in range(CM_GROUPS):
            zg = jnp.dot(wsp_ref[g], vb, preferred_element_type=F32)
            z = jnp.where((lane // HEAD_DIM) == g, zg, z)
        o_ref[0, rows, :] = (u * (z + bsp_ref[...])).astype(BF16)


def _cmlp(cuv, g_v, gones, w_sp, b_sp_lanes):
    bsz, tt, _ = cuv.shape
    tm = ROW_TILE
    return pl.pallas_call(
        _cmlp_kernel,
        grid=(bsz, tt // tm),
        in_specs=[pl.BlockSpec((1, tm, 2 * CM_W), lambda b, t: (b, t, 0)),
                  pl.BlockSpec((1, CM_W), lambda b, t: (0, 0)),
                  pl.BlockSpec((LANES, LANES), lambda b, t: (0, 0)),
                  pl.BlockSpec((CM_GROUPS, CM_CHUNK, CM_CHUNK), lambda b, t: (0, 0, 0)),
                  pl.BlockSpec((CM_CHUNK, CM_W), lambda b, t: (0, 0))],
        out_specs=pl.BlockSpec((1, tm, CM_W), lambda b, t: (b, t, 0)),
        out_shape=jax.ShapeDtypeStruct((bsz, tt, CM_W), BF16),
        compiler_params=_cparams("parallel", "parallel"),
        name="cmlp",
    )(cuv, g_v, gones, w_sp, b_sp_lanes)


def _out_kernel(at_ref, ml_ref, cm_ref, x_ref, mod_ref, g2_ref, w_ref, xo_ref, h2_ref):
    d = D_MODEL
    mod = mod_ref[0]
    g1, sh2, s2 = mod[:, 2 * d:3 * d], mod[:, 3 * d:4 * d], mod[:, 4 * d:5 * d]
    mm = (jnp.dot(at_ref[0], w_ref[0:ATTN_W, :], preferred_element_type=F32)
          + jnp.dot(ml_ref[0], w_ref[ATTN_W:ATTN_W + ML_W, :], preferred_element_type=F32)
          + jnp.dot(cm_ref[0], w_ref[ATTN_W + ML_W:, :], preferred_element_type=F32))
    y = x_ref[0] + g1 * mm
    xo_ref[0] = y
    ms = jnp.mean(y * y, axis=-1, keepdims=True)
    h2_ref[0] = ((y * lax.rsqrt(ms + EPS)) * g2_ref[...] * (1 + s2) + sh2).astype(BF16)


def _out_proj(attn, ml, cm, x_all, mod_l, g2, w_out, n_ctx_tiles):
    bsz, tt, d = x_all.shape
    tm = ROW_TILE
    row = lambda b, t: (b, t, 0)
    const = lambda b, t: (0, 0)
    return pl.pallas_call(
        _out_kernel,
        grid=(bsz, tt // tm),
        in_specs=[pl.BlockSpec((1, tm, ATTN_W), row),
                  pl.BlockSpec((1, tm, ML_W), row),
                  pl.BlockSpec((1, tm, CM_W), row),
                  pl.BlockSpec((1, tm, d), row),
                  pl.BlockSpec((1, 1, 6 * d), lambda b, t: (jnp.where(t < n_ctx_tiles, bsz, b), 0, 0)),
                  pl.BlockSpec((1, d), const),
                  pl.BlockSpec((d, d), const)],
        out_specs=[pl.BlockSpec((1, tm, d), row), pl.BlockSpec((1, tm, d), row)],
        out_shape=[jax.ShapeDtypeStruct((bsz, tt, d), F32), jax.ShapeDtypeStruct((bsz, tt, d), BF16)],
        compiler_params=_cparams("parallel", "parallel"),
        name="out_proj",
    )(attn, ml, cm, x_all, mod_l, g2, w_out)


def _ffn_kernel(h_ref, hp_ref, hn_ref, x_ref, mod_ref, wg_ref, wv_ref, cw_ref, wd_ref,
                o_ref, hcat_ref, acc_ref, *, n_ctx_tiles):
    d = D_MODEL
    tm = h_ref.shape[1]
    nf = wg_ref.shape[0]
    rows = tm + 2 * HALO
    t = pl.program_id(1)
    nt = pl.num_programs(1)
    has_prev = jnp.logical_and(t != 0, t != n_ctx_tiles)
    has_next = jnp.logical_and(t != n_ctx_tiles - 1, t != nt - 1)
    hp = hp_ref[0]
    hn = hn_ref[0]
    hcat_ref[0:HALO, :] = jnp.where(has_prev, hp, jnp.zeros_like(hp))
    hcat_ref[HALO:HALO + tm, :] = h_ref[0]
    hcat_ref[HALO + tm:rows, :] = jnp.where(has_next, hn, jnp.zeros_like(hn))
    acc_ref[...] = jnp.zeros_like(acc_ref)

    def body(f, carry):
        hc = hcat_ref[...]
        cw = cw_ref[f]
        halves = []
        for w_ref, off in ((wg_ref, 0), (wv_ref, FF_TILE)):
            a = jnp.dot(hc, w_ref[f], preferred_element_type=F32)
            prev = pltpu.roll(a, 1, axis=0)[HALO:HALO + tm]
            nxt = pltpu.roll(a, rows - 1, axis=0)[HALO:HALO + tm]
            cur = a[HALO:HALO + tm]
            sl = slice(off, off + FF_TILE)
            halves.append(prev * cw[0:1, sl] + cur * cw[1:2, sl] + nxt * cw[2:3, sl] + cw[3:4, sl])
        gate, val = halves
        act = (gate * jax.nn.sigmoid(gate)) * val
        acc_ref[...] += jnp.dot(act.astype(BF16), wd_ref[f], preferred_element_type=F32)
        return carry

    lax.fori_loop(0, nf, body, 0)
    g2 = mod_ref[0][:, 5 * d:6 * d]
    o_ref[0] = x_ref[0] + g2 * acc_ref[...]


def _conv_ffn(h2, x_mid, mod_l, wg, wv, cw, wd, n_ctx_tiles):
    bsz, tt, d = x_mid.shape
    tm = ROW_TILE
    nt = tt // tm
    nf, _, tf = wg.shape
    hb = tm // HALO
    n_halo_blocks = tt // HALO
    row = lambda b, t: (b, t, 0)
    c3 = lambda b, t: (0, 0, 0)
    kern = functools.partial(_ffn_kernel, n_ctx_tiles=n_ctx_tiles)
    return pl.pallas_call(
        kern,
        grid=(bsz, nt),
        in_specs=[pl.BlockSpec((1, tm, d), row),
                  pl.BlockSpec((1, HALO, d), lambda b, t: (b, jnp.maximum(t * hb - 1, 0), 0)),
                  pl.BlockSpec((1, HALO, d), lambda b, t: (b, jnp.minimum((t + 1) * hb, n_halo_blocks - 1), 0)),
                  pl.BlockSpec((1, tm, d), row),
                  pl.BlockSpec((1, 1, 6 * d), lambda b, t: (jnp.where(t < n_ctx_tiles, bsz, b), 0, 0)),
                  pl.BlockSpec((nf, d, tf), c3),
                  pl.BlockSpec((nf, d, tf), c3),
                  pl.BlockSpec((nf, 8, 2 * tf), c3),
                  pl.BlockSpec((nf, tf, d), c3)],
        out_specs=pl.BlockSpec((1, tm, d), row),
        out_shape=jax.ShapeDtypeStruct((bsz, tt, d), F32),
        scratch_shapes=[pltpu.VMEM((tm + 2 * HALO, d), BF16), pltpu.VMEM((tm, d), F32)],
        compiler_params=_cparams("parallel", "parallel"),
        name="conv_ffn",
    )(h2, h2, h2, x_mid, mod_l, wg, wv, cw, wd)


def _rope_tables(s_len, t_ctx):
    nf = HEAD_DIM // 4
    rows = s_len // GRID_W
    pos = jnp.arange(s_len)
    row = jnp.repeat(jnp.arange(rows), GRID_W).astype(F32)
    colp = (pos % GRID_W).astype(F32)
    inv = ROPE_THETA ** (-jnp.arange(nf, dtype=F32) / nf)
    ang_r, ang_c = row[:, None] * inv[None], colp[:, None] * inv[None]
    cr, cc, sr, sc = jnp.cos(ang_r), jnp.cos(ang_c), jnp.sin(ang_r), jnp.sin(ang_c)
    z = jnp.zeros_like(sr)
    cos = jnp.concatenate([cr, cr, cc, cc], axis=1)
    sa = jnp.concatenate([-sr, z, -sc, z], axis=1)
    sb = jnp.concatenate([z, sr, z, sc], axis=1)
    pad = lambda a, fill: jnp.concatenate([jnp.full((t_ctx, HEAD_DIM), fill, F32), a], axis=0)
    two = lambda a: jnp.concatenate([a, a], axis=1)
    return two(pad(cos, 1.0)), two(pad(sa, 0.0)), two(pad(sb, 0.0))


def kernel(x, c, ctx, c_ctx, w_ada, b_ada, g_norm1, w_in, b_in, g_q, g_k, g_mh, g_v, w_sp, b_sp,
           w_out, g_norm2, w_up, conv_w, conv_b, w_down):
    bsz, s_len, d = x.shape
    t_ctx = ctx.shape[1]
    depth = w_ada.shape[0]
    tt = t_ctx + s_len
    assert d == D_MODEL and t_ctx % ML_CHUNK == 0 and s_len % ML_CHUNK == 0 and t_ctx % ROW_TILE == 0
    n_ctx_tiles = t_ctx // ROW_TILE

    q_cols = np.concatenate([np.arange(h * HEAD_DIM, (h + 1) * HEAD_DIM) for h in Q_HEAD_ORDER])
    o = ATTN_W + 2 * KV_W
    gate_off = o + 4 * ML_W
    cm_off = gate_off + N_GATES
    cols = np.concatenate([q_cols, np.arange(ATTN_W, o + 4 * ML_W), np.arange(cm_off, cm_off + 2 * CM_W),
                           np.arange(gate_off, gate_off + N_GATES)])
    pad = N_PROJ - cols.shape[0]
    w_in_p = jnp.pad(w_in[:, :, cols], ((0, 0), (0, 0), (0, pad))).astype(BF16)
    b_in_p = jnp.pad(b_in[:, cols], ((0, 0), (0, pad)))[:, None, :]
    gqk = jnp.concatenate([jnp.tile(g_q, (1, N_Q_HEADS)) * (HEAD_DIM ** -0.5), jnp.tile(g_k, (1, N_KV_HEADS))],
                          axis=1)[:, None, :]
    w_out_p = jnp.concatenate([w_out[:, q_cols, :], w_out[:, ATTN_W:, :]], axis=1).astype(BF16)
    nf = D_FF // FF_TILE
    wg = w_up[:, :, :D_FF].reshape(depth, d, nf, FF_TILE).transpose(0, 2, 1, 3).astype(BF16)
    wv = w_up[:, :, D_FF:].reshape(depth, d, nf, FF_TILE).transpose(0, 2, 1, 3).astype(BF16)
    wd = w_down.reshape(depth, nf, FF_TILE, d).astype(BF16)
    taps = jnp.concatenate([conv_w, conv_b[:, None, :], jnp.zeros((depth, 4, 2 * D_FF), F32)], axis=1)
    cw = jnp.concatenate([taps[:, :, :D_FF].reshape(depth, 8, nf, FF_TILE),
                          taps[:, :, D_FF:].reshape(depth, 8, nf, FF_TILE)], axis=-1).transpose(0, 2, 1, 3)
    w_sp_b = w_sp.astype(BF16)
    b_sp_l = jnp.repeat(jnp.swapaxes(b_sp, 1, 2), HEAD_DIM, axis=2)
    blk = np.arange(LANES) // HEAD_DIM
    gones = jnp.asarray(blk[:, None] == blk[None, :], BF16)
    ii = np.arange(ML_CHUNK)
    tril = jnp.asarray(ii[None, :] <= ii[:, None], BF16)
    triu = jnp.asarray(ii[None, :] >= ii[:, None], BF16)
    cos_t, sa_t, sb_t = _rope_tables(s_len, t_ctx)

    n_rows = -(-(bsz + 1) // 8) * 8
    cs = jnp.concatenate([c, c_ctx[None, :], jnp.zeros((n_rows - bsz - 1, d), F32)], axis=0)
    mod = _modulation(cs, w_ada, b_ada)

    x_all = jnp.concatenate([ctx, x], axis=1)
    for l in range(depth):
        mod_l = mod[l][:, None, :]
        qk, v, mqkv, mo, cuv, gts = _in_proj(x_all, mod_l, g_norm1[l][None], w_in_p[l], b_in_p[l], gqk[l], gones,
                                             cos_t, sa_t, sb_t, n_ctx_tiles)
        attn = _attention(qk, v, t_ctx)
        gates_t = gts[:, :, :N_GATES].reshape(bsz, tt // ML_CHUNK, ML_CHUNK, N_GATES).transpose(0, 1, 3, 2)
        ml = _mlstm(mqkv, gts, gates_t, mo, g_mh[l][None], gones, tril, triu, t_ctx)
        cm = _cmlp(cuv, g_v[l][None], gones, w_sp_b[l], b_sp_l[l])
        x_mid, h2 = _out_proj(attn, ml, cm, x_all, mod_l, g_norm2[l][None], w_out_p[l], n_ctx_tiles)
        x_all = _conv_ffn(h2, x_mid, mod_l, wg[l], wv[l], cw[l], wd[l], n_ctx_tiles)
    return x_all[:, t_ctx:, :]
```

```python
import functools

import jax
import jax.numpy as jnp
import numpy as np
from jax import lax
from jax.experimental import pallas as pl
from jax.experimental.pallas import tpu as pltpu

F32 = jnp.float32
BF16 = jnp.bfloat16

D_MODEL = 1024
HEAD_DIM = 64
N_Q_HEADS = 8
N_KV_HEADS = 2
GRID_W = 64
ROPE_THETA = 10000.0
EPS = 1e-6
ATTN_W = N_Q_HEADS * HEAD_DIM
KV_W = N_KV_HEADS * HEAD_DIM
ML_HEADS = 4
ML_W = ML_HEADS * HEAD_DIM
CM_GROUPS = 4
CM_W = CM_GROUPS * HEAD_DIM
CM_CHUNK = 128
N_GATES = 4 * ML_HEADS
D_FF = 2816
LANES = 128

OFF_Q, OFF_K, OFF_V = 0, 512, 640
OFF_MQ, OFF_MO, OFF_CU, OFF_GT = 768, 1536, 1792, 2304
N_PROJ = 2432
QK_W = ATTN_W + KV_W

ROW_TILE = 256
ML_CHUNK = 256
FF_TILE = 256
HALO = 16
VMEM_LIMIT = 56 * 1024 * 1024

Q_HEAD_ORDER = (0, 4, 1, 5, 2, 6, 3, 7)


def _cparams(*sem):
    return pltpu.CompilerParams(dimension_semantics=sem, vmem_limit_bytes=VMEM_LIMIT)


def _group_mean_sq(x, g_ones):
    x2 = x * x
    hi = x2.astype(BF16)
    lo = (x2 - hi.astype(F32)).astype(BF16)
    cols = []
    for j in range(x.shape[1] // LANES):
        sl = slice(j * LANES, (j + 1) * LANES)
        cols.append(jnp.dot(hi[:, sl], g_ones, preferred_element_type=F32)
                    + jnp.dot(lo[:, sl], g_ones, preferred_element_type=F32))
    s = cols[0] if len(cols) == 1 else jnp.concatenate(cols, axis=1)
    return s * (1.0 / HEAD_DIM)


def _mod_kernel(c_ref, w_ref, b_ref, o_ref):
    c = c_ref[...]
    sc = c * jax.nn.sigmoid(c)
    o_ref[0] = jnp.dot(sc, w_ref[0], preferred_element_type=F32,
                       precision=lax.Precision.HIGHEST) + b_ref[0]


def _modulation(cs, w_ada, b_ada):
    depth, d, d6 = w_ada.shape
    r = cs.shape[0]
    nj = d6 // d
    return pl.pallas_call(
        _mod_kernel,
        grid=(depth, nj),
        in_specs=[pl.BlockSpec((r, d), lambda l, j: (0, 0)),
                  pl.BlockSpec((1, d, d), lambda l, j: (l, 0, j)),
                  pl.BlockSpec((1, 1, d), lambda l, j: (l, 0, j))],
        out_specs=pl.BlockSpec((1, r, d), lambda l, j: (l, 0, j)),
        out_shape=jax.ShapeDtypeStruct((depth, r, d6), F32),
        compiler_params=_cparams("arbitrary", "arbitrary"),
        name="adaln_mod",
    )(cs, w_ada, b_ada.reshape(depth, 1, d6))


def _in_kernel(x_ref, mod_ref, g1_ref, w_ref, b_ref, gqk_ref, gones_ref, cos_ref, sa_ref, sb_ref,
               qk_ref, v_ref, mqkv_ref, mo_ref, cuv_ref, gt_ref):
    d = D_MODEL
    x = x_ref[0]
    mod = mod_ref[0]
    sh1, s1 = mod[:, 0:d], mod[:, d:2 * d]
    ms = jnp.mean(x * x, axis=-1, keepdims=True)
    h = (x * lax.rsqrt(ms + EPS)) * g1_ref[...] * (1 + s1) + sh1
    p = jnp.dot(h.astype(BF16), w_ref[...], preferred_element_type=F32) + b_ref[...]

    qk = p[:, OFF_Q:OFF_Q + QK_W]
    gms = _group_mean_sq(qk, gones_ref[...])
    qn = (qk * lax.rsqrt(gms + EPS)) * gqk_ref[...]
    cos, sa, sb = cos_ref[...], sa_ref[...], sb_ref[...]
    for j in range(QK_W // LANES):
        xb = qn[:, j * LANES:(j + 1) * LANES]
        up = pltpu.roll(xb, LANES - 16, axis=1)
        dn = pltpu.roll(xb, 16, axis=1)
        qk_ref[0, :, j * LANES:(j + 1) * LANES] = (xb * cos + up * sa + dn * sb).astype(BF16)

    v_ref[0] = p[:, OFF_V:OFF_V + KV_W].astype(BF16)
    mqkv_ref[0, :, 0:ML_W] = p[:, OFF_MQ:OFF_MQ + ML_W].astype(BF16)
    mqkv_ref[0, :, ML_W:2 * ML_W] = (p[:, OFF_MQ + ML_W:OFF_MQ + 2 * ML_W] * (HEAD_DIM ** -0.5)).astype(BF16)
    mqkv_ref[0, :, 2 * ML_W:3 * ML_W] = p[:, OFF_MQ + 2 * ML_W:OFF_MQ + 3 * ML_W].astype(BF16)
    mo_ref[0] = p[:, OFF_MO:OFF_MO + ML_W]
    cuv_ref[0] = p[:, OFF_CU:OFF_CU + 2 * CM_W]
    gt_ref[0] = p[:, OFF_GT:OFF_GT + LANES]


def _in_proj(x_all, mod_l, g1, w_in, b_in, gqk, gones, cos_t, sa_t, sb_t, n_ctx_tiles):
    bsz, tt, d = x_all.shape
    tm = ROW_TILE
    nt = tt // tm
    row = lambda b, t: (b, t, 0)
    const = lambda b, t: (0, 0)
    tab = lambda b, t: (t, 0)
    outs = [(QK_W, BF16), (KV_W, BF16), (3 * ML_W, BF16), (ML_W, F32), (2 * CM_W, F32), (LANES, F32)]
    return pl.pallas_call(
        _in_kernel,
        grid=(bsz, nt),
        in_specs=[pl.BlockSpec((1, tm, d), row),
                  pl.BlockSpec((1, 1, 6 * d), lambda b, t: (jnp.where(t < n_ctx_tiles, bsz, b), 0, 0)),
                  pl.BlockSpec((1, d), const),
                  pl.BlockSpec((d, N_PROJ), const),
                  pl.BlockSpec((1, N_PROJ), const),
                  pl.BlockSpec((1, QK_W), const),
                  pl.BlockSpec((LANES, LANES), const),
                  pl.BlockSpec((tm, LANES), tab),
                  pl.BlockSpec((tm, LANES), tab),
                  pl.BlockSpec((tm, LANES), tab)],
        out_specs=[pl.BlockSpec((1, tm, w), row) for w, _ in outs],
        out_shape=[jax.ShapeDtypeStruct((bsz, tt, w), dt) for w, dt in outs],
        compiler_params=_cparams("parallel", "parallel"),
        name="in_proj",
    )(x_all, mod_l, g1, w_in, b_in, gqk, gones, cos_t, sa_t, sb_t)


def _attn_kernel(q_ref, k_ref, v_ref, o_ref, *, n_ctx_tiles, t_ctx):
    t = pl.program_id(1)
    tq = q_ref.shape[1]
    lane = lax.broadcasted_iota(jnp.int32, (tq, LANES), 1)
    low = lane < HEAD_DIM

    def attend(kv_len):
        k = k_ref[0, 0:kv_len, :]
        v = v_ref[0, 0:kv_len, :]
        for pair in range(ATTN_W // LANES):
            qp = q_ref[0, :, pair * LANES:(pair + 1) * LANES]
            halves = []
            for sel in (low, jnp.logical_not(low)):
                qh = jnp.where(sel, qp, jnp.zeros_like(qp))
                s = lax.dot_general(qh, k, (((1,), (1,)), ((), ())), preferred_element_type=F32)
                m = jnp.max(s, axis=-1, keepdims=True)
                e = jnp.exp(s - m)
                l = jnp.sum(e, axis=-1, keepdims=True)
                o = jnp.dot(e.astype(BF16), v, preferred_element_type=F32)
                halves.append(o / l)
            o_ref[0, :, pair * LANES:(pair + 1) * LANES] = jnp.where(low, halves[0], halves[1]).astype(BF16)

    @pl.when(t < n_ctx_tiles)
    def _():
        attend(t_ctx)

    @pl.when(t >= n_ctx_tiles)
    def _():
        attend(k_ref.shape[1])


def _attention(qk, v, t_ctx):
    bsz, tt, _ = qk.shape
    tq = ROW_TILE
    kern = functools.partial(_attn_kernel, n_ctx_tiles=t_ctx // tq, t_ctx=t_ctx)
    return pl.pallas_call(
        kern,
        grid=(bsz, tt // tq),
        in_specs=[pl.BlockSpec((1, tq, ATTN_W), lambda b, t: (b, t, 0)),
                  pl.BlockSpec((1, tt, KV_W), lambda b, t: (b, 0, ATTN_W // KV_W)),
                  pl.BlockSpec((1, tt, KV_W), lambda b, t: (b, 0, 0))],
        out_specs=pl.BlockSpec((1, tq, ATTN_W), lambda b, t: (b, t, 0)),
        out_shape=jax.ShapeDtypeStruct((bsz, tt, ATTN_W), BF16),
        compiler_params=_cparams("parallel", "parallel"),
        name="attention",
    )(qk, qk, v)


def _log_sigmoid(x):
    return jnp.minimum(x, 0.0) - jnp.log1p(jnp.exp(-jnp.abs(x)))


def _split3(a):
    hi = a.astype(BF16)
    r1 = a - hi.astype(F32)
    mid = r1.astype(BF16)
    lo = (r1 - mid.astype(F32)).astype(BF16)
    return hi, mid, lo


def _mlstm_kernel(qkv_ref, g_ref, gt_ref, mo_ref, gmh_ref, gones_ref, tril_ref, triu_ref,
                  o_ref, hf_ref, hb_ref, cn_ref, m_ref, *, n_ctx_chunks):
    L = ML_CHUNK
    W = ML_W
    CNW = W + LANES
    n_chunks = qkv_ref.shape[1] // L

    row = lax.broadcasted_iota(jnp.int32, (L, L), 0)
    col = lax.broadcasted_iota(jnp.int32, (L, L), 1)
    lane_w = lax.broadcasted_iota(jnp.int32, (L, W), 1)
    lane_cn = lax.broadcasted_iota(jnp.int32, (1, CNW), 1)
    lane_m = lax.broadcasted_iota(jnp.int32, (1, LANES), 1)
    cn_row = lax.broadcasted_iota(jnp.int32, (W, CNW), 0) // HEAD_DIM
    cn_col = lax.broadcasted_iota(jnp.int32, (W, CNW), 1)
    cn_mask = jnp.where(cn_col < W, cn_col // HEAD_DIM, cn_col - W) == cn_row
    ones_blk = jnp.ones((L, LANES), BF16)

    cn_ref[...] = jnp.zeros_like(cn_ref)
    m_ref[...] = jnp.zeros_like(m_ref)

    def step(rev, c):
        r0 = pl.multiple_of(c * L, L)
        q = qkv_ref[0, pl.ds(r0, L), 0:W]
        k = qkv_ref[0, pl.ds(r0, L), W:2 * W]
        v = qkv_ref[0, pl.ds(r0, L), 2 * W:3 * W]
        g = g_ref[0, pl.ds(r0, L), :]
        gt = gt_ref[0, c]
        goff = 2 * ML_HEADS if rev else 0
        tri_c = triu_ref[...] if rev else tril_ref[...]
        tri_r = tril_ref[...] if rev else triu_ref[...]
        valid = (col >= row) if rev else (col <= row)
        last = 0 if rev else L - 1

        bcols = sum(jnp.dot(tri_c, part, preferred_element_type=F32)
                    for part in _split3(_log_sigmoid(g)))
        brows = sum(jnp.dot(part, tri_r, preferred_element_type=F32)
                    for part in _split3(_log_sigmoid(gt)))

        cn = cn_ref[rev]
        m_all = m_ref[rev]
        qc = jnp.dot(q, cn.astype(BF16), preferred_element_type=F32)

        h_out = jnp.zeros((L, W), F32)
        gs_all = jnp.zeros((L, W), F32)
        wc_all = jnp.zeros((1, CNW), F32)
        m_new_all = jnp.zeros((1, LANES), F32)
        for h in range(ML_HEADS):
            ic, fc = goff + h, goff + ML_HEADS + h
            bcol = bcols[:, fc:fc + 1]
            brow = brows[fc:fc + 1, :]
            irow = gt[ic:ic + 1, :]
            icol = g[:, ic:ic + 1]
            head = (lane_w // HEAD_DIM) == h
            dm = jnp.where(valid, bcol - brow + irow, -jnp.inf)
            m_prev = m_all[0:1, h:h + 1]
            m_inter = bcol + m_prev
            m_t = jnp.maximum(m_inter, jnp.max(dm, axis=-1, keepdims=True))
            w_inter = jnp.exp(m_inter - m_t)
            qh = jnp.where(head, q, jnp.zeros_like(q))
            s = lax.dot_general(qh, k, (((1,), (1,)), ((), ())), preferred_element_type=F32)
            a = jnp.exp(dm - m_t) * s
            a_sum = jnp.sum(a, axis=-1, keepdims=True)
            av = jnp.dot(a.astype(BF16), v, preferred_element_type=F32)
            num = w_inter * qc[:, 0:W] + av
            den = w_inter * qc[:, W + h:W + h + 1] + a_sum
            hh = num / jnp.maximum(jnp.abs(den), jnp.exp(-m_t))
            h_out = jnp.where(head, hh, h_out)

            b_last = bcol[last:last + 1, :]
            m_new = m_t[last:last + 1, :]
            gs = jnp.exp(b_last - bcol + icol - m_new)
            wc = jnp.exp(b_last + m_prev - m_new)
            gs_all = jnp.where(head, gs, gs_all)
            cn_head = jnp.where(lane_cn < W, lane_cn // HEAD_DIM, lane_cn - W) == h
            wc_all = jnp.where(cn_head, wc, wc_all)
            m_new_all = jnp.where(lane_m == h, m_new, m_new_all)

        gk = (k.astype(F32) * gs_all).astype(BF16)
        v1 = jnp.concatenate([v, ones_blk], axis=1)
        upd = lax.dot_general(gk, v1, (((0,), (0,)), ((), ())), preferred_element_type=F32)
        cn_ref[rev] = wc_all * cn + jnp.where(cn_mask, upd, 0.0)
        m_ref[rev] = jnp.broadcast_to(m_new_all, m_all.shape)
        dst = hb_ref if rev else hf_ref
        dst[pl.ds(r0, L), :] = h_out

    def body(j, carry):
        step(0, j)
        cb = jnp.where(j < n_ctx_chunks, n_ctx_chunks - 1 - j, n_chunks - 1 - (j - n_ctx_chunks))
        step(1, cb)
        return carry

    lax.fori_loop(0, n_chunks, body, 0)

    def merge(c, carry):
        r0 = pl.multiple_of(c * L, L)
        hs = hf_ref[pl.ds(r0, L), :] + hb_ref[pl.ds(r0, L), :]
        gms = _group_mean_sq(hs, gones_ref[...])
        y = (hs * lax.rsqrt(gms + EPS)) * gmh_ref[...]
        o_ref[0, pl.ds(r0, L), :] = (jax.nn.sigmoid(mo_ref[0, pl.ds(r0, L), :]) * y).astype(BF16)
        return carry

    lax.fori_loop(0, n_chunks, merge, 0)


def _mlstm(mqkv, gates, gates_t, mo, g_mh, gones, tril, triu, t_ctx):
    bsz, tt, _ = mqkv.shape
    L = ML_CHUNK
    nc = tt // L
    kern = functools.partial(_mlstm_kernel, n_ctx_chunks=t_ctx // L)
    per_b = lambda b: (b, 0, 0)
    const = lambda b: (0, 0)
    return pl.pallas_call(
        kern,
        grid=(bsz,),
        in_specs=[pl.BlockSpec((1, tt, 3 * ML_W), per_b),
                  pl.BlockSpec((1, tt, LANES), per_b),
                  pl.BlockSpec((1, nc, N_GATES, L), lambda b: (b, 0, 0, 0)),
                  pl.BlockSpec((1, tt, ML_W), per_b),
                  pl.BlockSpec((1, ML_W), const),
                  pl.BlockSpec((LANES, LANES), const),
                  pl.BlockSpec((L, L), const),
                  pl.BlockSpec((L, L), const)],
        out_specs=pl.BlockSpec((1, tt, ML_W), per_b),
        out_shape=jax.ShapeDtypeStruct((bsz, tt, ML_W), BF16),
        scratch_shapes=[pltpu.VMEM((tt, ML_W), F32), pltpu.VMEM((tt, ML_W), F32),
                        pltpu.VMEM((2, ML_W, ML_W + LANES), F32), pltpu.VMEM((2, 8, LANES), F32)],
        compiler_params=_cparams("parallel"),
        name="mlstm",
    )(mqkv, gates, gates_t, mo, g_mh, gones, tril, triu)


def _cmlp_kernel(uv_ref, gv_ref, gones_ref, wsp_ref, bsp_ref, o_ref):
    n_chunks = uv_ref.shape[1] // CM_CHUNK
    lane = lax.broadcasted_iota(jnp.int32, (CM_CHUNK, CM_W), 1)
    for c in range(n_chunks):
        rows = slice(c * CM_CHUNK, (c + 1) * CM_CHUNK)
        u = jax.nn.gelu(uv_ref[0, rows, 0:CM_W])
        v = jax.nn.gelu(uv_ref[0, rows, CM_W:2 * CM_W])
        gms = _group_mean_sq(v, gones_ref[...])
        vb = ((v * lax.rsqrt(gms + EPS)) * gv_ref[...]).astype(BF16)
        z = jnp.zeros((CM_CHUNK, CM_W), F32)
        for g in range(CM_GROUPS):
            zg = jnp.dot(wsp_ref[g], vb, preferred_element_type=F32)
            z = jnp.where((lane // HEAD_DIM) == g, zg, z)
        o_ref[0, rows, :] = (u * (z + bsp_ref[...])).astype(BF16)


def _cmlp(cuv, g_v, gones, w_sp, b_sp_lanes):
    bsz, tt, _ = cuv.shape
    tm = ROW_TILE
    return pl.pallas_call(
        _cmlp_kernel,
        grid=(bsz, tt // tm),
        in_specs=[pl.BlockSpec((1, tm, 2 * CM_W), lambda b, t: (b, t, 0)),
                  pl.BlockSpec((1, CM_W), lambda b, t: (0, 0)),
                  pl.BlockSpec((LANES, LANES), lambda b, t: (0, 0)),
                  pl.BlockSpec((CM_GROUPS, CM_CHUNK, CM_CHUNK), lambda b, t: (0, 0, 0)),
                  pl.BlockSpec((CM_CHUNK, CM_W), lambda b, t: (0, 0))],
        out_specs=pl.BlockSpec((1, tm, CM_W), lambda b, t: (b, t, 0)),
        out_shape=jax.ShapeDtypeStruct((bsz, tt, CM_W), BF16),
        compiler_params=_cparams("parallel", "parallel"),
        name="cmlp",
    )(cuv, g_v, gones, w_sp, b_sp_lanes)


def _out_kernel(at_ref, ml_ref, cm_ref, x_ref, mod_ref, g2_ref, w_ref, xo_ref, h2_ref):
    d = D_MODEL
    mod = mod_ref[0]
    g1, sh2, s2 = mod[:, 2 * d:3 * d], mod[:, 3 * d:4 * d], mod[:, 4 * d:5 * d]
    mm = (jnp.dot(at_ref[0], w_ref[0:ATTN_W, :], preferred_element_type=F32)
          + jnp.dot(ml_ref[0], w_ref[ATTN_W:ATTN_W + ML_W, :], preferred_element_type=F32)
          + jnp.dot(cm_ref[0], w_ref[ATTN_W + ML_W:, :], preferred_element_type=F32))
    y = x_ref[0] + g1 * mm
    xo_ref[0] = y
    ms = jnp.mean(y * y, axis=-1, keepdims=True)
    h2_ref[0] = ((y * lax.rsqrt(ms + EPS)) * g2_ref[...] * (1 + s2) + sh2).astype(BF16)


def _out_proj(attn, ml, cm, x_all, mod_l, g2, w_out, n_ctx_tiles):
    bsz, tt, d = x_all.shape
    tm = ROW_TILE
    row = lambda b, t: (b, t, 0)
    const = lambda b, t: (0, 0)
    return pl.pallas_call(
        _out_kernel,
        grid=(bsz, tt // tm),
        in_specs=[pl.BlockSpec((1, tm, ATTN_W), row),
                  pl.BlockSpec((1, tm, ML_W), row),
                  pl.BlockSpec((1, tm, CM_W), row),
                  pl.BlockSpec((1, tm, d), row),
                  pl.BlockSpec((1, 1, 6 * d), lambda b, t: (jnp.where(t < n_ctx_tiles, bsz, b), 0, 0)),
                  pl.BlockSpec((1, d), const),
                  pl.BlockSpec((d, d), const)],
        out_specs=[pl.BlockSpec((1, tm, d), row), pl.BlockSpec((1, tm, d), row)],
        out_shape=[jax.ShapeDtypeStruct((bsz, tt, d), F32), jax.ShapeDtypeStruct((bsz, tt, d), BF16)],
        compiler_params=_cparams("parallel", "parallel"),
        name="out_proj",
    )(attn, ml, cm, x_all, mod_l, g2, w_out)


def _ffn_kernel(h_ref, hp_ref, hn_ref, x_ref, mod_ref, wu_ref, cw_ref, wd_ref, o_ref, hcat_ref, act_ref, *,
                n_ctx_tiles):
    d = D_MODEL
    tm = h_ref.shape[1]
    t = pl.program_id(1)
    nt = pl.num_programs(1)
    has_prev = jnp.logical_and(t != 0, t != n_ctx_tiles)
    has_next = jnp.logical_and(t != n_ctx_tiles - 1, t != nt - 1)
    hrow = lax.broadcasted_iota(jnp.int32, (HALO, d), 0)
    before = pltpu.roll(hp_ref[0].astype(F32), 1, axis=0)
    after = pltpu.roll(hn_ref[0].astype(F32), 7, axis=0)
    halo = jnp.where(jnp.logical_and(hrow == 0, has_prev), before,
                     jnp.where(jnp.logical_and(hrow == 7, has_next), after, 0.0))
    hcat_ref[0:tm, :] = h_ref[0]
    hcat_ref[tm:tm + HALO, :] = halo.astype(BF16)
    hc = hcat_ref[...]
    sub = lax.broadcasted_iota(jnp.int32, (8, FF_TILE), 0)

    for f in range(D_FF // FF_TILE):
        halves = []
        for off in (f * FF_TILE, D_FF + f * FF_TILE):
            sl = slice(off, off + FF_TILE)
            a = jnp.dot(hc, wu_ref[:, sl], preferred_element_type=F32)
            cur = a[0:tm]
            edge = a[tm:tm + 8]
            dn = pltpu.roll(cur, 1, axis=0)
            up = pltpu.roll(cur, tm - 1, axis=0)
            prev = jnp.concatenate([jnp.where(sub == 0, edge, dn[0:8]), dn[8:]], axis=0)
            nxt = jnp.concatenate([up[:tm - 8], jnp.where(sub == 7, edge, up[tm - 8:])], axis=0)
            halves.append(prev * cw_ref[0:1, sl] + cur * cw_ref[1:2, sl] + nxt * cw_ref[2:3, sl] + cw_ref[3:4, sl])
        gate, val = halves
        act_ref[:, f * FF_TILE:(f + 1) * FF_TILE] = ((gate * jax.nn.sigmoid(gate)) * val).astype(BF16)
    acc = jnp.dot(act_ref[...], wd_ref[...], preferred_element_type=F32)
    g2 = mod_ref[0][:, 5 * d:6 * d]
    o_ref[0] = x_ref[0] + g2 * acc


def _conv_ffn(h2, x_mid, mod_l, w_up, cw, w_down, n_ctx_tiles):
    bsz, tt, d = x_mid.shape
    tm = ROW_TILE
    nt = tt // tm
    hb = tm // HALO
    n_halo_blocks = tt // HALO
    row = lambda b, t: (b, t, 0)
    const = lambda b, t: (0, 0)
    kern = functools.partial(_ffn_kernel, n_ctx_tiles=n_ctx_tiles)
    return pl.pallas_call(
        kern,
        grid=(bsz, nt),
        in_specs=[pl.BlockSpec((1, tm, d), row),
                  pl.BlockSpec((1, HALO, d), lambda b, t: (b, jnp.maximum(t * hb - 1, 0), 0)),
                  pl.BlockSpec((1, HALO, d), lambda b, t: (b, jnp.minimum((t + 1) * hb, n_halo_blocks - 1), 0)),
                  pl.BlockSpec((1, tm, d), row),
                  pl.BlockSpec((1, 1, 6 * d), lambda b, t: (jnp.where(t < n_ctx_tiles, bsz, b), 0, 0)),
                  pl.BlockSpec((d, 2 * D_FF), const),
                  pl.BlockSpec((8, 2 * D_FF), const),
                  pl.BlockSpec((D_FF, d), const)],
        out_specs=pl.BlockSpec((1, tm, d), row),
        out_shape=jax.ShapeDtypeStruct((bsz, tt, d), F32),
        scratch_shapes=[pltpu.VMEM((tm + HALO, d), BF16), pltpu.VMEM((tm, D_FF), BF16)],
        compiler_params=_cparams("parallel", "parallel"),
        name="conv_ffn",
    )(h2, h2, h2, x_mid, mod_l, w_up, cw, w_down)


def _rope_tables(s_len, t_ctx):
    nf = HEAD_DIM // 4
    rows = s_len // GRID_W
    pos = jnp.arange(s_len)
    row = jnp.repeat(jnp.arange(rows), GRID_W).astype(F32)
    colp = (pos % GRID_W).astype(F32)
    inv = ROPE_THETA ** (-jnp.arange(nf, dtype=F32) / nf)
    ang_r, ang_c = row[:, None] * inv[None], colp[:, None] * inv[None]
    cr, cc, sr, sc = jnp.cos(ang_r), jnp.cos(ang_c), jnp.sin(ang_r), jnp.sin(ang_c)
    z = jnp.zeros_like(sr)
    cos = jnp.concatenate([cr, cr, cc, cc], axis=1)
    sa = jnp.concatenate([-sr, z, -sc, z], axis=1)
    sb = jnp.concatenate([z, sr, z, sc], axis=1)
    pad = lambda a, fill: jnp.concatenate([jnp.full((t_ctx, HEAD_DIM), fill, F32), a], axis=0)
    two = lambda a: jnp.concatenate([a, a], axis=1)
    return two(pad(cos, 1.0)), two(pad(sa, 0.0)), two(pad(sb, 0.0))


def kernel(x, c, ctx, c_ctx, w_ada, b_ada, g_norm1, w_in, b_in, g_q, g_k, g_mh, g_v, w_sp, b_sp,
           w_out, g_norm2, w_up, conv_w, conv_b, w_down):
    bsz, s_len, d = x.shape
    t_ctx = ctx.shape[1]
    depth = w_ada.shape[0]
    tt = t_ctx + s_len
    assert d == D_MODEL and t_ctx % ML_CHUNK == 0 and s_len % ML_CHUNK == 0 and t_ctx % ROW_TILE == 0
    n_ctx_tiles = t_ctx // ROW_TILE

    q_cols = np.concatenate([np.arange(h * HEAD_DIM, (h + 1) * HEAD_DIM) for h in Q_HEAD_ORDER])
    o = ATTN_W + 2 * KV_W
    gate_off = o + 4 * ML_W
    cm_off = gate_off + N_GATES
    cols = np.concatenate([q_cols, np.arange(ATTN_W, o + 4 * ML_W), np.arange(cm_off, cm_off + 2 * CM_W),
                           np.arange(gate_off, gate_off + N_GATES)])
    pad = N_PROJ - cols.shape[0]
    w_in_p = jnp.pad(w_in[:, :, cols], ((0, 0), (0, 0), (0, pad))).astype(BF16)
    b_in_p = jnp.pad(b_in[:, cols], ((0, 0), (0, pad)))[:, None, :]
    gqk = jnp.concatenate([jnp.tile(g_q, (1, N_Q_HEADS)) * (HEAD_DIM ** -0.5), jnp.tile(g_k, (1, N_KV_HEADS))],
                          axis=1)[:, None, :]
    w_out_p = jnp.concatenate([w_out[:, q_cols, :], w_out[:, ATTN_W:, :]], axis=1).astype(BF16)
    w_up_b = w_up.astype(BF16)
    w_down_b = w_down.astype(BF16)
    cw = jnp.concatenate([conv_w, conv_b[:, None, :], jnp.zeros((depth, 4, 2 * D_FF), F32)], axis=1)
    w_sp_b = w_sp.astype(BF16)
    b_sp_l = jnp.repeat(jnp.swapaxes(b_sp, 1, 2), HEAD_DIM, axis=2)
    blk = np.arange(LANES) // HEAD_DIM
    gones = jnp.asarray(blk[:, None] == blk[None, :], BF16)
    ii = np.arange(ML_CHUNK)
    tril = jnp.asarray(ii[None, :] <= ii[:, None], BF16)
    triu = jnp.asarray(ii[None, :] >= ii[:, None], BF16)
    cos_t, sa_t, sb_t = _rope_tables(s_len, t_ctx)

    n_rows = -(-(bsz + 1) // 8) * 8
    cs = jnp.concatenate([c, c_ctx[None, :], jnp.zeros((n_rows - bsz - 1, d), F32)], axis=0)
    mod = _modulation(cs, w_ada, b_ada)

    x_all = jnp.concatenate([ctx, x], axis=1)
    for l in range(depth):
        mod_l = mod[l][:, None, :]
        qk, v, mqkv, mo, cuv, gts = _in_proj(x_all, mod_l, g_norm1[l][None], w_in_p[l], b_in_p[l], gqk[l], gones,
                                             cos_t, sa_t, sb_t, n_ctx_tiles)
        attn = _attention(qk, v, t_ctx)
        gates_t = gts[:, :, :N_GATES].reshape(bsz, tt // ML_CHUNK, ML_CHUNK, N_GATES).transpose(0, 1, 3, 2)
        ml = _mlstm(mqkv, gts, gates_t, mo, g_mh[l][None], gones, tril, triu, t_ctx)
        cm = _cmlp(cuv, g_v[l][None], gones, w_sp_b[l], b_sp_l[l])
        x_mid, h2 = _out_proj(attn, ml, cm, x_all, mod_l, g_norm2[l][None], w_out_p[l], n_ctx_tiles)
        x_all = _conv_ffn(h2, x_mid, mod_l, w_up_b[l], cw[l], w_down_b[l], n_ctx_tiles)
    return x_all[:, t_ctx:, :]
```

```python
import functools

import jax
import jax.numpy as jnp
import numpy as np
from jax import lax
from jax.experimental import pallas as pl
from jax.experimental.pallas import tpu as pltpu

F32 = jnp.float32
BF16 = jnp.bfloat16

D_MODEL = 1024
HEAD_DIM = 64
N_Q_HEADS = 8
N_KV_HEADS = 2
GRID_W = 64
ROPE_THETA = 10000.0
EPS = 1e-6
ATTN_W = N_Q_HEADS * HEAD_DIM
KV_W = N_KV_HEADS * HEAD_DIM
ML_HEADS = 4
ML_W = ML_HEADS * HEAD_DIM
CM_GROUPS = 4
CM_W = CM_GROUPS * HEAD_DIM
CM_CHUNK = 128
N_GATES = 4 * ML_HEADS
D_FF = 2816
LANES = 128

OFF_Q, OFF_K, OFF_V = 0, 512, 640
OFF_MQ, OFF_MO, OFF_CU, OFF_GT = 768, 1536, 1792, 2304
N_PROJ = 2432
QK_W = ATTN_W + KV_W

ROW_TILE = 256
ML_CHUNK = 256
FF_TILE = 256
HALO = 16
VMEM_LIMIT = 56 * 1024 * 1024

Q_HEAD_ORDER = (0, 4, 1, 5, 2, 6, 3, 7)


def _cparams(*sem):
    return pltpu.CompilerParams(dimension_semantics=sem, vmem_limit_bytes=VMEM_LIMIT)


def _group_mean_sq(x, g_ones):
    x2 = x * x
    hi = x2.astype(BF16)
    lo = (x2 - hi.astype(F32)).astype(BF16)
    cols = []
    for j in range(x.shape[1] // LANES):
        sl = slice(j * LANES, (j + 1) * LANES)
        cols.append(jnp.dot(hi[:, sl], g_ones, preferred_element_type=F32)
                    + jnp.dot(lo[:, sl], g_ones, preferred_element_type=F32))
    s = cols[0] if len(cols) == 1 else jnp.concatenate(cols, axis=1)
    return s * (1.0 / HEAD_DIM)


def _mod_kernel(c_ref, w_ref, b_ref, o_ref):
    c = c_ref[...]
    sc = c * jax.nn.sigmoid(c)
    o_ref[0] = jnp.dot(sc, w_ref[0], preferred_element_type=F32,
                       precision=lax.Precision.HIGHEST) + b_ref[0]


def _modulation(cs, w_ada, b_ada):
    depth, d, d6 = w_ada.shape
    r = cs.shape[0]
    nj = d6 // d
    return pl.pallas_call(
        _mod_kernel,
        grid=(depth, nj),
        in_specs=[pl.BlockSpec((r, d), lambda l, j: (0, 0)),
                  pl.BlockSpec((1, d, d), lambda l, j: (l, 0, j)),
                  pl.BlockSpec((1, 1, d), lambda l, j: (l, 0, j))],
        out_specs=pl.BlockSpec((1, r, d), lambda l, j: (l, 0, j)),
        out_shape=jax.ShapeDtypeStruct((depth, r, d6), F32),
        compiler_params=_cparams("arbitrary", "arbitrary"),
        name="adaln_mod",
    )(cs, w_ada, b_ada.reshape(depth, 1, d6))


def _in_kernel(x_ref, mod_ref, g1_ref, w_ref, b_ref, gqk_ref, gones_ref, cos_ref, sa_ref, sb_ref,
               qk_ref, v_ref, mqkv_ref, mo_ref, cuv_ref, gt_ref):
    d = D_MODEL
    x = x_ref[0]
    mod = mod_ref[0]
    sh1, s1 = mod[:, 0:d], mod[:, d:2 * d]
    ms = jnp.mean(x * x, axis=-1, keepdims=True)
    h = (x * lax.rsqrt(ms + EPS)) * g1_ref[...] * (1 + s1) + sh1
    p = jnp.dot(h.astype(BF16), w_ref[...], preferred_element_type=F32) + b_ref[...]

    qk = p[:, OFF_Q:OFF_Q + QK_W]
    gms = _group_mean_sq(qk, gones_ref[...])
    qn = (qk * lax.rsqrt(gms + EPS)) * gqk_ref[...]
    cos, sa, sb = cos_ref[...], sa_ref[...], sb_ref[...]
    for j in range(QK_W // LANES):
        xb = qn[:, j * LANES:(j + 1) * LANES]
        up = pltpu.roll(xb, LANES - 16, axis=1)
        dn = pltpu.roll(xb, 16, axis=1)
        qk_ref[0, :, j * LANES:(j + 1) * LANES] = (xb * cos + up * sa + dn * sb).astype(BF16)

    v_ref[0] = p[:, OFF_V:OFF_V + KV_W].astype(BF16)
    mqkv_ref[0, :, 0:ML_W] = p[:, OFF_MQ:OFF_MQ + ML_W].astype(BF16)
    mqkv_ref[0, :, ML_W:2 * ML_W] = (p[:, OFF_MQ + ML_W:OFF_MQ + 2 * ML_W] * (HEAD_DIM ** -0.5)).astype(BF16)
    mqkv_ref[0, :, 2 * ML_W:3 * ML_W] = p[:, OFF_MQ + 2 * ML_W:OFF_MQ + 3 * ML_W].astype(BF16)
    mo_ref[0] = p[:, OFF_MO:OFF_MO + ML_W]
    cuv_ref[0] = p[:, OFF_CU:OFF_CU + 2 * CM_W]
    gt_ref[0] = p[:, OFF_GT:OFF_GT + LANES]


def _in_proj(x_all, mod_l, g1, w_in, b_in, gqk, gones, cos_t, sa_t, sb_t, n_ctx_tiles):
    bsz, tt, d = x_all.shape
    tm = ROW_TILE
    nt = tt // tm
    row = lambda b, t: (b, t, 0)
    const = lambda b, t: (0, 0)
    tab = lambda b, t: (t, 0)
    outs = [(QK_W, BF16), (KV_W, BF16), (3 * ML_W, BF16), (ML_W, F32), (2 * CM_W, F32), (LANES, F32)]
    return pl.pallas_call(
        _in_kernel,
        grid=(bsz, nt),
        in_specs=[pl.BlockSpec((1, tm, d), row),
                  pl.BlockSpec((1, 1, 6 * d), lambda b, t: (jnp.where(t < n_ctx_tiles, bsz, b), 0, 0)),
                  pl.BlockSpec((1, d), const),
                  pl.BlockSpec((d, N_PROJ), const),
                  pl.BlockSpec((1, N_PROJ), const),
                  pl.BlockSpec((1, QK_W), const),
                  pl.BlockSpec((LANES, LANES), const),
                  pl.BlockSpec((tm, LANES), tab),
                  pl.BlockSpec((tm, LANES), tab),
                  pl.BlockSpec((tm, LANES), tab)],
        out_specs=[pl.BlockSpec((1, tm, w), row) for w, _ in outs],
        out_shape=[jax.ShapeDtypeStruct((bsz, tt, w), dt) for w, dt in outs],
        compiler_params=_cparams("parallel", "parallel"),
        name="in_proj",
    )(x_all, mod_l, g1, w_in, b_in, gqk, gones, cos_t, sa_t, sb_t)


def _attn_kernel(q_ref, k_ref, v_ref, o_ref, *, n_ctx_tiles, t_ctx):
    t = pl.program_id(1)
    tq = q_ref.shape[1]
    low = lax.broadcasted_iota(jnp.int32, (tq, LANES), 1) < HEAD_DIM

    def attend(kv_len):
        k = k_ref[0, 0:kv_len, :]
        v = v_ref[0, 0:kv_len, :]
        for pair in range(ATTN_W // LANES):
            qp = q_ref[0, :, pair * LANES:(pair + 1) * LANES]
            halves = []
            for sel in (low, jnp.logical_not(low)):
                qh = jnp.where(sel, qp, jnp.zeros_like(qp))
                s = lax.dot_general(qh, k, (((1,), (1,)), ((), ())), preferred_element_type=F32)
                m = jnp.max(s, axis=-1, keepdims=True)
                e = jnp.exp(s - m)
                l = jnp.sum(e, axis=-1, keepdims=True)
                o = jnp.dot(e.astype(BF16), v, preferred_element_type=F32)
                halves.append(o / l)
            o_ref[0, :, pair * LANES:(pair + 1) * LANES] = jnp.where(low, halves[0], halves[1]).astype(BF16)

    @pl.when(t < n_ctx_tiles)
    def _():
        attend(t_ctx)

    @pl.when(t >= n_ctx_tiles)
    def _():
        attend(k_ref.shape[1])


def _attention(qk, v, t_ctx):
    bsz, tt, _ = qk.shape
    tq = ROW_TILE
    kern = functools.partial(_attn_kernel, n_ctx_tiles=t_ctx // tq, t_ctx=t_ctx)
    return pl.pallas_call(
        kern,
        grid=(bsz, tt // tq),
        in_specs=[pl.BlockSpec((1, tq, ATTN_W), lambda b, t: (b, t, 0)),
                  pl.BlockSpec((1, tt, KV_W), lambda b, t: (b, 0, ATTN_W // KV_W)),
                  pl.BlockSpec((1, tt, KV_W), lambda b, t: (b, 0, 0))],
        out_specs=pl.BlockSpec((1, tq, ATTN_W), lambda b, t: (b, t, 0)),
        out_shape=jax.ShapeDtypeStruct((bsz, tt, ATTN_W), BF16),
        compiler_params=_cparams("parallel", "parallel"),
        name="attention",
    )(qk, qk, v)


def _log_sigmoid(x):
    return jnp.minimum(x, 0.0) - jnp.log1p(jnp.exp(-jnp.abs(x)))


def _split3(a):
    hi = a.astype(BF16)
    r1 = a - hi.astype(F32)
    mid = r1.astype(BF16)
    lo = (r1 - mid.astype(F32)).astype(BF16)
    return hi, mid, lo


def _mlstm_kernel(qkv_ref, gt_ref, mo_ref, gmh_ref, gones_ref, esel_ref, tril_ref, triu_ref,
                  o_ref, hf_ref, hb_ref, cn_ref, m_ref, neg_ref, cmask_ref, *, n_ctx_chunks):
    L = ML_CHUNK
    W = ML_W
    n_chunks = qkv_ref.shape[1] // L

    row = lax.broadcasted_iota(jnp.int32, (L, L), 0)
    col = lax.broadcasted_iota(jnp.int32, (L, L), 1)
    neg_ref[0] = jnp.where(col <= row, 0.0, -jnp.inf)
    neg_ref[1] = jnp.where(col >= row, 0.0, -jnp.inf)
    blk_r = lax.broadcasted_iota(jnp.int32, (W, 2 * W), 0) // HEAD_DIM
    blk_c = (lax.broadcasted_iota(jnp.int32, (W, 2 * W), 1) % W) // HEAD_DIM
    cmask_ref[...] = (blk_r == blk_c).astype(F32)
    low = lax.broadcasted_iota(jnp.int32, (L, LANES), 1) < HEAD_DIM
    ones_blk = jnp.ones((L, W), BF16)
    cn_ref[...] = jnp.zeros_like(cn_ref)
    m_ref[...] = jnp.zeros_like(m_ref)

    def step(rev, c):
        r0 = pl.multiple_of(c * L, L)
        q = qkv_ref[0, pl.ds(r0, L), 0:W]
        k = qkv_ref[0, pl.ds(r0, L), W:2 * W]
        v = qkv_ref[0, pl.ds(r0, L), 2 * W:3 * W]
        gt = gt_ref[0, c]
        goff = 2 * ML_HEADS if rev else 0
        tri_r = tril_ref[...] if rev else triu_ref[...]
        last = 0 if rev else L - 1

        brows = sum(jnp.dot(part, tri_r, preferred_element_type=F32) for part in _split3(_log_sigmoid(gt)))
        bexp = sum(lax.dot_general(part, esel_ref[rev], (((0,), (0,)), ((), ())), preferred_element_type=F32)
                   for part in _split3(brows))
        r_rows = gt[goff:goff + ML_HEADS] - brows[goff + ML_HEADS:goff + 2 * ML_HEADS]

        cn = cn_ref[rev]
        m_prev = m_ref[rev][0:1, :]
        qc = jnp.dot(q, cn.astype(BF16), preferred_element_type=F32)
        kt = k.astype(F32).T
        kt_b = kt.astype(BF16)

        mus, asums, avs, gsr = [], [], [], []
        for h in range(ML_HEADS):
            cols = slice((h // 2) * LANES, (h // 2 + 1) * LANES)
            sel = low if h % 2 == 0 else jnp.logical_not(low)
            qh = jnp.where(sel, q[:, cols], jnp.zeros((L, LANES), BF16))
            s = jnp.dot(qh, kt_b[cols, :], preferred_element_type=F32)
            r = r_rows[h:h + 1, :]
            rm = r + neg_ref[rev]
            mu = jnp.maximum(jnp.max(rm, axis=-1, keepdims=True), m_prev[:, h * HEAD_DIM:h * HEAD_DIM + 1])
            a = jnp.exp(rm - mu) * s
            asums.append(jnp.sum(a, axis=-1, keepdims=True))
            avs.append(jnp.dot(a.astype(BF16), v[:, cols], preferred_element_type=F32))
            mus.append(mu)
            gsr.append(jnp.broadcast_to(jnp.exp(r - mu[last:last + 1, :]), (HEAD_DIM, L)))

        pair = lambda xs: jnp.concatenate([jnp.where(low, xs[0], xs[1]), jnp.where(low, xs[2], xs[3])], axis=1)
        mu_x, asum_x, av_x = pair(mus), pair(asums), pair(avs)
        w_inter = jnp.exp(m_prev - mu_x)
        m_t = bexp + mu_x
        num = w_inter * qc[:, 0:W] + av_x
        den = w_inter * qc[:, W:2 * W] + asum_x
        dst = hb_ref if rev else hf_ref
        dst[pl.ds(r0, L), :] = num / jnp.maximum(jnp.abs(den), jnp.exp(-m_t))

        mu_last = mu_x[last:last + 1, :]
        wc = jnp.exp(m_prev - mu_last)
        gkt = (kt * jnp.concatenate(gsr, axis=0)).astype(BF16)
        upd = jnp.dot(gkt, jnp.concatenate([v, ones_blk], axis=1), preferred_element_type=F32)
        cn_ref[rev] = jnp.concatenate([wc, wc], axis=1) * cn + upd * cmask_ref[...]
        m_ref[rev] = jnp.broadcast_to(bexp[last:last + 1, :] + mu_last, (8, W))

    def body(j, carry):
        step(0, j)
        cb = jnp.where(j < n_ctx_chunks, n_ctx_chunks - 1 - j, n_chunks - 1 - (j - n_ctx_chunks))
        step(1, cb)
        return carry

    lax.fori_loop(0, n_chunks, body, 0)

    def merge(c, carry):
        r0 = pl.multiple_of(c * L, L)
        hs = hf_ref[pl.ds(r0, L), :] + hb_ref[pl.ds(r0, L), :]
        gms = _group_mean_sq(hs, gones_ref[...])
        y = (hs * lax.rsqrt(gms + EPS)) * gmh_ref[...]
        o_ref[0, pl.ds(r0, L), :] = (jax.nn.sigmoid(mo_ref[0, pl.ds(r0, L), :]) * y).astype(BF16)
        return carry

    lax.fori_loop(0, n_chunks, merge, 0)


def _mlstm(mqkv, gates_t, mo, g_mh, gones, esel, tril, triu, t_ctx):
    bsz, tt, _ = mqkv.shape
    L = ML_CHUNK
    nc = tt // L
    kern = functools.partial(_mlstm_kernel, n_ctx_chunks=t_ctx // L)
    per_b = lambda b: (b, 0, 0)
    const = lambda b: (0, 0)
    return pl.pallas_call(
        kern,
        grid=(bsz,),
        in_specs=[pl.BlockSpec((1, tt, 3 * ML_W), per_b),
                  pl.BlockSpec((1, nc, N_GATES, L), lambda b: (b, 0, 0, 0)),
                  pl.BlockSpec((1, tt, ML_W), per_b),
                  pl.BlockSpec((1, ML_W), const),
                  pl.BlockSpec((LANES, LANES), const),
                  pl.BlockSpec((2, N_GATES, ML_W), lambda b: (0, 0, 0)),
                  pl.BlockSpec((L, L), const),
                  pl.BlockSpec((L, L), const)],
        out_specs=pl.BlockSpec((1, tt, ML_W), per_b),
        out_shape=jax.ShapeDtypeStruct((bsz, tt, ML_W), BF16),
        scratch_shapes=[pltpu.VMEM((tt, ML_W), F32), pltpu.VMEM((tt, ML_W), F32),
                        pltpu.VMEM((2, ML_W, 2 * ML_W), F32), pltpu.VMEM((2, 8, ML_W), F32),
                        pltpu.VMEM((2, L, L), F32), pltpu.VMEM((ML_W, 2 * ML_W), F32)],
        compiler_params=_cparams("parallel"),
        name="mlstm",
    )(mqkv, gates_t, mo, g_mh, gones, esel, tril, triu)


def _cmlp_kernel(uv_ref, gv_ref, gones_ref, wsp_ref, bsp_ref, o_ref):
    n_chunks = uv_ref.shape[1] // CM_CHUNK
    lane = lax.broadcasted_iota(jnp.int32, (CM_CHUNK, CM_W), 1)
    for c in range(n_chunks):
        rows = slice(c * CM_CHUNK, (c + 1) * CM_CHUNK)
        u = jax.nn.gelu(uv_ref[0, rows, 0:CM_W])
        v = jax.nn.gelu(uv_ref[0, rows, CM_W:2 * CM_W])
        gms = _group_mean_sq(v, gones_ref[...])
        vb = ((v * lax.rsqrt(gms + EPS)) * gv_ref[...]).astype(BF16)
        z = jnp.zeros((CM_CHUNK, CM_W), F32)
        for g in range(CM_GROUPS):
            zg = jnp.dot(wsp_ref[g], vb, preferred_element_type=F32)
            z = jnp.where((lane // HEAD_DIM) == g, zg, z)
        o_ref[0, rows, :] = (u * (z + bsp_ref[...])).astype(BF16)


def _cmlp(cuv, g_v, gones, w_sp, b_sp_lanes):
    bsz, tt, _ = cuv.shape
    tm = ROW_TILE
    return pl.pallas_call(
        _cmlp_kernel,
        grid=(bsz, tt // tm),
        in_specs=[pl.BlockSpec((1, tm, 2 * CM_W), lambda b, t: (b, t, 0)),
                  pl.BlockSpec((1, CM_W), lambda b, t: (0, 0)),
                  pl.BlockSpec((LANES, LANES), lambda b, t: (0, 0)),
                  pl.BlockSpec((CM_GROUPS, CM_CHUNK, CM_CHUNK), lambda b, t: (0, 0, 0)),
                  pl.BlockSpec((CM_CHUNK, CM_W), lambda b, t: (0, 0))],
        out_specs=pl.BlockSpec((1, tm, CM_W), lambda b, t: (b, t, 0)),
        out_shape=jax.ShapeDtypeStruct((bsz, tt, CM_W), BF16),
        compiler_params=_cparams("parallel", "parallel"),
        name="cmlp",
    )(cuv, g_v, gones, w_sp, b_sp_lanes)


def _out_kernel(at_ref, ml_ref, cm_ref, x_ref, mod_ref, g2_ref, w_ref, xo_ref, h2_ref):
    d = D_MODEL
    mod = mod_ref[0]
    g1, sh2, s2 = mod[:, 2 * d:3 * d], mod[:, 3 * d:4 * d], mod[:, 4 * d:5 * d]
    mm = (jnp.dot(at_ref[0], w_ref[0:ATTN_W, :], preferred_element_type=F32)
          + jnp.dot(ml_ref[0], w_ref[ATTN_W:ATTN_W + ML_W, :], preferred_element_type=F32)
          + jnp.dot(cm_ref[0], w_ref[ATTN_W + ML_W:, :], preferred_element_type=F32))
    y = x_ref[0] + g1 * mm
    xo_ref[0] = y
    ms = jnp.mean(y * y, axis=-1, keepdims=True)
    h2_ref[0] = ((y * lax.rsqrt(ms + EPS)) * g2_ref[...] * (1 + s2) + sh2).astype(BF16)


def _out_proj(attn, ml, cm, x_all, mod_l, g2, w_out, n_ctx_tiles, skip_tiles):
    bsz, tt, d = x_all.shape
    tm = ROW_TILE
    t_out = tt - skip_tiles * tm
    row = lambda b, t: (b, t + skip_tiles, 0)
    out_row = lambda b, t: (b, t, 0)
    const = lambda b, t: (0, 0)
    return pl.pallas_call(
        _out_kernel,
        grid=(bsz, t_out // tm),
        in_specs=[pl.BlockSpec((1, tm, ATTN_W), row),
                  pl.BlockSpec((1, tm, ML_W), row),
                  pl.BlockSpec((1, tm, CM_W), row),
                  pl.BlockSpec((1, tm, d), row),
                  pl.BlockSpec((1, 1, 6 * d), lambda b, t: (jnp.where(t + skip_tiles < n_ctx_tiles, bsz, b), 0, 0)),
                  pl.BlockSpec((1, d), const),
                  pl.BlockSpec((d, d), const)],
        out_specs=[pl.BlockSpec((1, tm, d), out_row), pl.BlockSpec((1, tm, d), out_row)],
        out_shape=[jax.ShapeDtypeStruct((bsz, t_out, d), F32), jax.ShapeDtypeStruct((bsz, t_out, d), BF16)],
        compiler_params=_cparams("parallel", "parallel"),
        name="out_proj",
    )(attn, ml, cm, x_all, mod_l, g2, w_out)


def _ffn_kernel(h_ref, hp_ref, hn_ref, x_ref, mod_ref, wu_ref, cw_ref, wd_ref, o_ref, hcat_ref, act_ref, *,
                n_ctx_tiles):
    d = D_MODEL
    tm = h_ref.shape[1]
    t = pl.program_id(1)
    nt = pl.num_programs(1)
    has_prev = jnp.logical_and(t != 0, t != n_ctx_tiles)
    has_next = jnp.logical_and(t != n_ctx_tiles - 1, t != nt - 1)
    hrow = lax.broadcasted_iota(jnp.int32, (HALO, d), 0)
    before = pltpu.roll(hp_ref[0].astype(F32), 1, axis=0)
    after = pltpu.roll(hn_ref[0].astype(F32), 7, axis=0)
    halo = jnp.where(jnp.logical_and(hrow == 0, has_prev), before,
                     jnp.where(jnp.logical_and(hrow == 7, has_next), after, 0.0))
    hcat_ref[0:tm, :] = h_ref[0]
    hcat_ref[tm:tm + HALO, :] = halo.astype(BF16)
    hc = hcat_ref[...]
    sub = lax.broadcasted_iota(jnp.int32, (8, FF_TILE), 0)

    for f in range(D_FF // FF_TILE):
        halves = []
        for off in (f * FF_TILE, D_FF + f * FF_TILE):
            sl = slice(off, off + FF_TILE)
            a = jnp.dot(hc, wu_ref[:, sl], preferred_element_type=F32)
            cur = a[0:tm]
            edge = a[tm:tm + 8]
            dn = pltpu.roll(cur, 1, axis=0)
            up = pltpu.roll(cur, tm - 1, axis=0)
            prev = jnp.concatenate([jnp.where(sub == 0, edge, dn[0:8]), dn[8:]], axis=0)
            nxt = jnp.concatenate([up[:tm - 8], jnp.where(sub == 7, edge, up[tm - 8:])], axis=0)
            halves.append(prev * cw_ref[0:1, sl] + cur * cw_ref[1:2, sl] + nxt * cw_ref[2:3, sl] + cw_ref[3:4, sl])
        gate, val = halves
        act_ref[:, f * FF_TILE:(f + 1) * FF_TILE] = ((gate * jax.nn.sigmoid(gate)) * val).astype(BF16)
    acc = jnp.dot(act_ref[...], wd_ref[...], preferred_element_type=F32)
    g2 = mod_ref[0][:, 5 * d:6 * d]
    o_ref[0] = x_ref[0] + g2 * acc


def _conv_ffn(h2, x_mid, mod_l, w_up, cw, w_down, n_ctx_tiles):
    bsz, tt, d = x_mid.shape
    tm = ROW_TILE
    nt = tt // tm
    hb = tm // HALO
    n_halo_blocks = tt // HALO
    row = lambda b, t: (b, t, 0)
    const = lambda b, t: (0, 0)
    kern = functools.partial(_ffn_kernel, n_ctx_tiles=n_ctx_tiles)
    return pl.pallas_call(
        kern,
        grid=(bsz, nt),
        in_specs=[pl.BlockSpec((1, tm, d), row),
                  pl.BlockSpec((1, HALO, d), lambda b, t: (b, jnp.maximum(t * hb - 1, 0), 0)),
                  pl.BlockSpec((1, HALO, d), lambda b, t: (b, jnp.minimum((t + 1) * hb, n_halo_blocks - 1), 0)),
                  pl.BlockSpec((1, tm, d), row),
                  pl.BlockSpec((1, 1, 6 * d), lambda b, t: (jnp.where(t < n_ctx_tiles, bsz, b), 0, 0)),
                  pl.BlockSpec((d, 2 * D_FF), const),
                  pl.BlockSpec((8, 2 * D_FF), const),
                  pl.BlockSpec((D_FF, d), const)],
        out_specs=pl.BlockSpec((1, tm, d), row),
        out_shape=jax.ShapeDtypeStruct((bsz, tt, d), F32),
        scratch_shapes=[pltpu.VMEM((tm + HALO, d), BF16), pltpu.VMEM((tm, D_FF), BF16)],
        compiler_params=_cparams("parallel", "parallel"),
        name="conv_ffn",
    )(h2, h2, h2, x_mid, mod_l, w_up, cw, w_down)


def _rope_tables(s_len, t_ctx):
    nf = HEAD_DIM // 4
    rows = s_len // GRID_W
    pos = jnp.arange(s_len)
    row = jnp.repeat(jnp.arange(rows), GRID_W).astype(F32)
    colp = (pos % GRID_W).astype(F32)
    inv = ROPE_THETA ** (-jnp.arange(nf, dtype=F32) / nf)
    ang_r, ang_c = row[:, None] * inv[None], colp[:, None] * inv[None]
    cr, cc, sr, sc = jnp.cos(ang_r), jnp.cos(ang_c), jnp.sin(ang_r), jnp.sin(ang_c)
    z = jnp.zeros_like(sr)
    cos = jnp.concatenate([cr, cr, cc, cc], axis=1)
    sa = jnp.concatenate([-sr, z, -sc, z], axis=1)
    sb = jnp.concatenate([z, sr, z, sc], axis=1)
    pad = lambda a, fill: jnp.concatenate([jnp.full((t_ctx, HEAD_DIM), fill, F32), a], axis=0)
    two = lambda a: jnp.concatenate([a, a], axis=1)
    return two(pad(cos, 1.0)), two(pad(sa, 0.0)), two(pad(sb, 0.0))


def kernel(x, c, ctx, c_ctx, w_ada, b_ada, g_norm1, w_in, b_in, g_q, g_k, g_mh, g_v, w_sp, b_sp,
           w_out, g_norm2, w_up, conv_w, conv_b, w_down):
    bsz, s_len, d = x.shape
    t_ctx = ctx.shape[1]
    depth = w_ada.shape[0]
    tt = t_ctx + s_len
    assert d == D_MODEL and t_ctx % ML_CHUNK == 0 and s_len % ML_CHUNK == 0 and t_ctx % ROW_TILE == 0
    n_ctx_tiles = t_ctx // ROW_TILE

    o = ATTN_W + 2 * KV_W
    gate_off = o + 4 * ML_W
    cm_off = gate_off + N_GATES
    segs = ([(h * HEAD_DIM, (h + 1) * HEAD_DIM) for h in Q_HEAD_ORDER]
            + [(ATTN_W, gate_off), (cm_off, cm_off + 2 * CM_W), (gate_off, gate_off + N_GATES)])
    pad = N_PROJ - sum(e - s for s, e in segs)
    w_in_p = jnp.concatenate([w_in[:, :, s:e].astype(BF16) for s, e in segs]
                             + [jnp.zeros((depth, d, pad), BF16)], axis=2)
    b_in_p = jnp.concatenate([b_in[:, s:e] for s, e in segs] + [jnp.zeros((depth, pad), F32)], axis=1)[:, None, :]
    gqk = jnp.concatenate([jnp.tile(g_q, (1, N_Q_HEADS)) * (HEAD_DIM ** -0.5), jnp.tile(g_k, (1, N_KV_HEADS))],
                          axis=1)[:, None, :]
    w_out_p = jnp.concatenate([w_out[:, h * HEAD_DIM:(h + 1) * HEAD_DIM, :].astype(BF16) for h in Q_HEAD_ORDER]
                              + [w_out[:, ATTN_W:, :].astype(BF16)], axis=1)
    w_up_b = w_up.astype(BF16)
    w_down_b = w_down.astype(BF16)
    cw = jnp.concatenate([conv_w, conv_b[:, None, :], jnp.zeros((depth, 4, 2 * D_FF), F32)], axis=1)
    w_sp_b = w_sp.astype(BF16)
    b_sp_l = jnp.repeat(jnp.swapaxes(b_sp, 1, 2), HEAD_DIM, axis=2)
    blk = np.arange(LANES) // HEAD_DIM
    gones = jnp.asarray(blk[:, None] == blk[None, :], BF16)
    ii = np.arange(ML_CHUNK)
    tril = jnp.asarray(ii[None, :] <= ii[:, None], BF16)
    triu = jnp.asarray(ii[None, :] >= ii[:, None], BF16)
    gate_row = np.arange(N_GATES)[:, None]
    lane_head = (np.arange(ML_W) // HEAD_DIM)[None, :]
    esel = jnp.asarray(np.stack([gate_row == ML_HEADS + lane_head, gate_row == 3 * ML_HEADS + lane_head]), BF16)
    cos_t, sa_t, sb_t = _rope_tables(s_len, t_ctx)

    n_rows = -(-(bsz + 1) // 8) * 8
    cs = jnp.concatenate([c, c_ctx[None, :], jnp.zeros((n_rows - bsz - 1, d), F32)], axis=0)
    mod = _modulation(cs, w_ada, b_ada)

    x_all = jnp.concatenate([ctx, x], axis=1)
    for l in range(depth):
        mod_l = mod[l][:, None, :]
        qk, v, mqkv, mo, cuv, gts = _in_proj(x_all, mod_l, g_norm1[l][None], w_in_p[l], b_in_p[l], gqk[l], gones,
                                             cos_t, sa_t, sb_t, n_ctx_tiles)
        attn = _attention(qk, v, t_ctx)
        gates_t = gts[:, :, :N_GATES].reshape(bsz, tt // ML_CHUNK, ML_CHUNK, N_GATES).transpose(0, 1, 3, 2)
        ml = _mlstm(mqkv, gates_t, mo, g_mh[l][None], gones, esel, tril, triu, t_ctx)
        cm = _cmlp(cuv, g_v[l][None], gones, w_sp_b[l], b_sp_l[l])
        skip = n_ctx_tiles if l == depth - 1 else 0
        x_mid, h2 = _out_proj(attn, ml, cm, x_all, mod_l, g_norm2[l][None], w_out_p[l], n_ctx_tiles, skip)
        x_all = _conv_ffn(h2, x_mid, mod_l, w_up_b[l], cw[l], w_down_b[l], n_ctx_tiles - skip)
    return x_all
```

```python
import functools

import jax
import jax.numpy as jnp
import numpy as np
from jax import lax
from jax.experimental import pallas as pl
from jax.experimental.pallas import tpu as pltpu

F32 = jnp.float32
BF16 = jnp.bfloat16

D_MODEL = 1024
HEAD_DIM = 64
N_Q_HEADS = 8
N_KV_HEADS = 2
GRID_W = 64
ROPE_THETA = 10000.0
EPS = 1e-6
ATTN_W = N_Q_HEADS * HEAD_DIM
KV_W = N_KV_HEADS * HEAD_DIM
ML_HEADS = 4
ML_W = ML_HEADS * HEAD_DIM
CM_GROUPS = 4
CM_W = CM_GROUPS * HEAD_DIM
CM_CHUNK = 128
N_GATES = 4 * ML_HEADS
D_FF = 2816
LANES = 128

OFF_Q, OFF_K, OFF_V = 0, 512, 640
OFF_MQ, OFF_MO, OFF_CU, OFF_GT = 768, 1536, 1792, 2304
N_PROJ = 2432
QK_W = ATTN_W + KV_W

ROW_TILE = 256
ML_CHUNK = 256
FF_TILE = 256
HALO = 16
ATTN_CHUNK = 128
LOG2_E = 1.4426950408889634
VMEM_LIMIT = 56 * 1024 * 1024

Q_HEAD_ORDER = (0, 4, 1, 5, 2, 6, 3, 7)


def _cparams(*sem):
    return pltpu.CompilerParams(dimension_semantics=sem, vmem_limit_bytes=VMEM_LIMIT)


def _group_mean_sq(x, g_ones):
    x2 = x * x
    hi = x2.astype(BF16)
    lo = (x2 - hi.astype(F32)).astype(BF16)
    g2 = jnp.concatenate([g_ones, g_ones], axis=0)
    cols = []
    for j in range(x.shape[1] // LANES):
        sl = slice(j * LANES, (j + 1) * LANES)
        cols.append(jnp.dot(jnp.concatenate([hi[:, sl], lo[:, sl]], axis=1), g2, preferred_element_type=F32))
    s = cols[0] if len(cols) == 1 else jnp.concatenate(cols, axis=1)
    return s * (1.0 / HEAD_DIM)


def _mod_kernel(c_ref, w_ref, b_ref, o_ref):
    c = c_ref[...]
    sc = c * jax.nn.sigmoid(c)
    o_ref[0] = jnp.dot(sc, w_ref[0], preferred_element_type=F32,
                       precision=lax.Precision.HIGHEST) + b_ref[0]


def _modulation(cs, w_ada, b_ada):
    depth, d, d6 = w_ada.shape
    r = cs.shape[0]
    nj = d6 // d
    return pl.pallas_call(
        _mod_kernel,
        grid=(depth, nj),
        in_specs=[pl.BlockSpec((r, d), lambda l, j: (0, 0)),
                  pl.BlockSpec((1, d, d), lambda l, j: (l, 0, j)),
                  pl.BlockSpec((1, 1, d), lambda l, j: (l, 0, j))],
        out_specs=pl.BlockSpec((1, r, d), lambda l, j: (l, 0, j)),
        out_shape=jax.ShapeDtypeStruct((depth, r, d6), F32),
        compiler_params=_cparams("arbitrary", "arbitrary"),
        name="adaln_mod",
    )(cs, w_ada, b_ada.reshape(depth, 1, d6))


def _in_kernel(x_ref, mod_ref, g1_ref, w_ref, b_ref, gqk_ref, gones_ref, cos_ref, sa_ref, sb_ref,
               qk_ref, v_ref, mqkv_ref, mo_ref, cuv_ref, gt_ref):
    d = D_MODEL
    x = x_ref[0]
    mod = mod_ref[0]
    sh1, s1 = mod[:, 0:d], mod[:, d:2 * d]
    ms = jnp.mean(x * x, axis=-1, keepdims=True)
    h = (x * lax.rsqrt(ms + EPS)) * g1_ref[...] * (1 + s1) + sh1
    p = jnp.dot(h.astype(BF16), w_ref[...], preferred_element_type=F32) + b_ref[...]

    qk = p[:, OFF_Q:OFF_Q + QK_W]
    gms = _group_mean_sq(qk, gones_ref[...])
    qn = (qk * lax.rsqrt(gms + EPS)) * gqk_ref[...]
    cos, sa, sb = cos_ref[...], sa_ref[...], sb_ref[...]
    for j in range(QK_W // LANES):
        xb = qn[:, j * LANES:(j + 1) * LANES]
        up = pltpu.roll(xb, LANES - 16, axis=1)
        dn = pltpu.roll(xb, 16, axis=1)
        qk_ref[0, :, j * LANES:(j + 1) * LANES] = (xb * cos + up * sa + dn * sb).astype(BF16)

    v_ref[0] = p[:, OFF_V:OFF_V + KV_W].T.astype(BF16)
    mqkv_ref[0, :, 0:ML_W] = p[:, OFF_MQ:OFF_MQ + ML_W].astype(BF16)
    mqkv_ref[0, :, ML_W:2 * ML_W] = (p[:, OFF_MQ + ML_W:OFF_MQ + 2 * ML_W] * (HEAD_DIM ** -0.5)).astype(BF16)
    mqkv_ref[0, :, 2 * ML_W:3 * ML_W] = p[:, OFF_MQ + 2 * ML_W:OFF_MQ + 3 * ML_W].astype(BF16)
    mo_ref[0] = p[:, OFF_MO:OFF_MO + ML_W]
    cuv_ref[0] = p[:, OFF_CU:OFF_CU + 2 * CM_W]
    gt_ref[0] = p[:, OFF_GT:OFF_GT + LANES]


def _in_proj(x_all, mod_l, g1, w_in, b_in, gqk, gones, cos_t, sa_t, sb_t, n_ctx_tiles):
    bsz, tt, d = x_all.shape
    tm = ROW_TILE
    nt = tt // tm
    row = lambda b, t: (b, t, 0)
    const = lambda b, t: (0, 0)
    tab = lambda b, t: (t, 0)
    outs = [(QK_W, BF16), (KV_W, BF16), (3 * ML_W, BF16), (ML_W, F32), (2 * CM_W, F32), (LANES, F32)]
    return pl.pallas_call(
        _in_kernel,
        grid=(bsz, nt),
        in_specs=[pl.BlockSpec((1, tm, d), row),
                  pl.BlockSpec((1, 1, 6 * d), lambda b, t: (jnp.where(t < n_ctx_tiles, bsz, b), 0, 0)),
                  pl.BlockSpec((1, d), const),
                  pl.BlockSpec((d, N_PROJ), const),
                  pl.BlockSpec((1, N_PROJ), const),
                  pl.BlockSpec((1, QK_W), const),
                  pl.BlockSpec((LANES, LANES), const),
                  pl.BlockSpec((tm, LANES), tab),
                  pl.BlockSpec((tm, LANES), tab),
                  pl.BlockSpec((tm, LANES), tab)],
        out_specs=[pl.BlockSpec((1, KV_W, tm), lambda b, t: (b, 0, t)) if i == 1 else pl.BlockSpec((1, tm, w), row)
                   for i, (w, _) in enumerate(outs)],
        out_shape=[jax.ShapeDtypeStruct((bsz, KV_W, tt) if i == 1 else (bsz, tt, w), dt)
                   for i, (w, dt) in enumerate(outs)],
        compiler_params=_cparams("parallel", "parallel"),
        name="in_proj",
    )(x_all, mod_l, g1, w_in, b_in, gqk, gones, cos_t, sa_t, sb_t)


def _attn_kernel(q_ref, k_ref, vt_ref, o_ref, s_ref, e_ref, *, n_ctx_tiles, t_ctx):
    t = pl.program_id(1)
    tq = q_ref.shape[1]
    low = lax.broadcasted_iota(jnp.int32, (tq, LANES), 1) < HEAD_DIM
    n_pairs = ATTN_W // LANES
    groups = [(kvh, pp) for pp in range(n_pairs // 2) for kvh in range(2)]

    def attend(kv_len):
        def scores(g):
            k = k_ref[0, 0:kv_len, :]
            kvh, pp = groups[g]
            sel = low if kvh == 0 else jnp.logical_not(low)
            q2 = jnp.concatenate(
                [jnp.where(sel, q_ref[0, :, p * LANES:(p + 1) * LANES], jnp.zeros((tq, LANES), BF16))
                 for p in (2 * pp, 2 * pp + 1)], axis=0)
            s_ref[g % 2, 0:kv_len, :] = lax.dot_general(k, q2, (((1,), (1,)), ((), ())),
                                                        preferred_element_type=F32)

        scores(0)
        outs = {}
        for g, (kvh, pp) in enumerate(groups):
            if g + 1 < len(groups):
                scores(g + 1)
            chunks = [slice(c, c + ATTN_CHUNK) for c in range(0, kv_len, ATTN_CHUNK)]
            fold = lambda a: a.reshape(ATTN_CHUNK // 8, 8, 2 * tq)
            m8 = None
            for rows in chunks:
                mc = jnp.max(fold(s_ref[g % 2, rows, :]), axis=0)
                m8 = mc if m8 is None else jnp.maximum(m8, mc)
            m = jnp.max(m8, axis=0, keepdims=True)
            for rows in chunks:
                e_ref[g % 2, rows, :] = jnp.exp2(s_ref[g % 2, rows, :] - m).astype(BF16)
            vt = vt_ref[0, kvh * HEAD_DIM:(kvh + 1) * HEAD_DIM, 0:kv_len]
            vt1 = jnp.concatenate([vt, jnp.ones((16, kv_len), BF16)], axis=0)
            ov = jnp.dot(vt1, e_ref[g % 2, 0:kv_len, :], preferred_element_type=F32)
            outs[kvh] = ov[0:HEAD_DIM] / ov[HEAD_DIM:HEAD_DIM + 1]
            if kvh == 1:
                for j, p in enumerate((2 * pp, 2 * pp + 1)):
                    both = jnp.concatenate([outs[0][:, j * tq:(j + 1) * tq], outs[1][:, j * tq:(j + 1) * tq]], axis=0)
                    o_ref[0, :, p * LANES:(p + 1) * LANES] = both.T.astype(BF16)

    @pl.when(t < n_ctx_tiles)
    def _():
        attend(t_ctx)

    @pl.when(t >= n_ctx_tiles)
    def _():
        attend(k_ref.shape[1])


def _attention(qk, vt, t_ctx):
    bsz, tt, _ = qk.shape
    tq = ROW_TILE
    kern = functools.partial(_attn_kernel, n_ctx_tiles=t_ctx // tq, t_ctx=t_ctx)
    return pl.pallas_call(
        kern,
        grid=(bsz, tt // tq),
        in_specs=[pl.BlockSpec((1, tq, ATTN_W), lambda b, t: (b, t, 0)),
                  pl.BlockSpec((1, tt, KV_W), lambda b, t: (b, 0, ATTN_W // KV_W)),
                  pl.BlockSpec((1, KV_W, tt), lambda b, t: (b, 0, 0))],
        out_specs=pl.BlockSpec((1, tq, ATTN_W), lambda b, t: (b, t, 0)),
        out_shape=jax.ShapeDtypeStruct((bsz, tt, ATTN_W), BF16),
        scratch_shapes=[pltpu.VMEM((2, tt, 2 * tq), F32), pltpu.VMEM((2, tt, 2 * tq), BF16)],
        compiler_params=_cparams("parallel", "parallel"),
        name="attention",
    )(qk, qk, vt)


def _log_sigmoid(x):
    return jnp.minimum(x, 0.0) - jnp.log1p(jnp.exp(-jnp.abs(x)))


def _split3(a):
    hi = a.astype(BF16)
    r1 = a - hi.astype(F32)
    mid = r1.astype(BF16)
    lo = (r1 - mid.astype(F32)).astype(BF16)
    return hi, mid, lo


def _mlstm_kernel(qkv_ref, gt_ref, mo_ref, gmh_ref, gones_ref, esel_ref, tril_ref, triu_ref,
                  o_ref, hf_ref, hb_ref, cn_ref, m_ref, neg_ref, cmask_ref, *, n_ctx_chunks):
    L = ML_CHUNK
    W = ML_W
    n_chunks = qkv_ref.shape[1] // L

    row = lax.broadcasted_iota(jnp.int32, (L, L), 0)
    col = lax.broadcasted_iota(jnp.int32, (L, L), 1)
    neg_ref[0] = jnp.where(col <= row, 0.0, -jnp.inf)
    neg_ref[1] = jnp.where(col >= row, 0.0, -jnp.inf)
    blk_r = lax.broadcasted_iota(jnp.int32, (W, 2 * W), 0) // HEAD_DIM
    blk_c = (lax.broadcasted_iota(jnp.int32, (W, 2 * W), 1) % W) // HEAD_DIM
    cmask_ref[...] = (blk_r == blk_c).astype(F32)
    low = lax.broadcasted_iota(jnp.int32, (L, LANES), 1) < HEAD_DIM
    ones_blk = jnp.ones((L, W), BF16)
    cn_ref[...] = jnp.zeros_like(cn_ref)
    m_ref[...] = jnp.zeros_like(m_ref)

    def step(rev, c):
        r0 = pl.multiple_of(c * L, L)
        q = qkv_ref[0, pl.ds(r0, L), 0:W]
        k = qkv_ref[0, pl.ds(r0, L), W:2 * W]
        v = qkv_ref[0, pl.ds(r0, L), 2 * W:3 * W]
        gt = gt_ref[0, c]
        goff = 2 * ML_HEADS if rev else 0
        tri_r = tril_ref[...] if rev else triu_ref[...]
        last = 0 if rev else L - 1

        brows = sum(jnp.dot(part, tri_r, preferred_element_type=F32) for part in _split3(_log_sigmoid(gt)))
        bexp = sum(lax.dot_general(part, esel_ref[rev], (((0,), (0,)), ((), ())), preferred_element_type=F32)
                   for part in _split3(brows))
        r_rows = gt[goff:goff + ML_HEADS] - brows[goff + ML_HEADS:goff + 2 * ML_HEADS]

        cn = cn_ref[rev]
        m_prev = m_ref[rev][0:1, :]
        qc = jnp.dot(q, cn.astype(BF16), preferred_element_type=F32)
        kt = k.astype(F32).T
        kt_b = kt.astype(BF16)

        mus, asums, avs, gsr = [], [], [], []
        for h in range(ML_HEADS):
            cols = slice((h // 2) * LANES, (h // 2 + 1) * LANES)
            sel = low if h % 2 == 0 else jnp.logical_not(low)
            qh = jnp.where(sel, q[:, cols], jnp.zeros((L, LANES), BF16))
            s = jnp.dot(qh, kt_b[cols, :], preferred_element_type=F32)
            r = r_rows[h:h + 1, :]
            rm = r + neg_ref[rev]
            mu = jnp.maximum(jnp.max(rm, axis=-1, keepdims=True), m_prev[:, h * HEAD_DIM:h * HEAD_DIM + 1])
            a = jnp.exp(rm - mu) * s
            asums.append(jnp.sum(a, axis=-1, keepdims=True))
            avs.append(jnp.dot(a.astype(BF16), v[:, cols], preferred_element_type=F32))
            mus.append(mu)
            gsr.append(jnp.broadcast_to(jnp.exp(r - mu[last:last + 1, :]), (HEAD_DIM, L)))

        pair = lambda xs: jnp.concatenate([jnp.where(low, xs[0], xs[1]), jnp.where(low, xs[2], xs[3])], axis=1)
        mu_x, asum_x, av_x = pair(mus), pair(asums), pair(avs)
        w_inter = jnp.exp(m_prev - mu_x)
        m_t = bexp + mu_x
        num = w_inter * qc[:, 0:W] + av_x
        den = w_inter * qc[:, W:2 * W] + asum_x
        dst = hb_ref if rev else hf_ref
        dst[pl.ds(r0, L), :] = num / jnp.maximum(jnp.abs(den), jnp.exp(-m_t))

        mu_last = mu_x[last:last + 1, :]
        wc = jnp.exp(m_prev - mu_last)
        gkt = (kt * jnp.concatenate(gsr, axis=0)).astype(BF16)
        upd = jnp.dot(gkt, jnp.concatenate([v, ones_blk], axis=1), preferred_element_type=F32)
        cn_ref[rev] = jnp.concatenate([wc, wc], axis=1) * cn + upd * cmask_ref[...]
        m_ref[rev] = jnp.broadcast_to(bexp[last:last + 1, :] + mu_last, (8, W))

    def body(j, carry):
        step(0, j)
        cb = jnp.where(j < n_ctx_chunks, n_ctx_chunks - 1 - j, n_chunks - 1 - (j - n_ctx_chunks))
        step(1, cb)
        return carry

    lax.fori_loop(0, n_chunks, body, 0)

    def merge(c, carry):
        r0 = pl.multiple_of(c * L, L)
        hs = hf_ref[pl.ds(r0, L), :] + hb_ref[pl.ds(r0, L), :]
        gms = _group_mean_sq(hs, gones_ref[...])
        y = (hs * lax.rsqrt(gms + EPS)) * gmh_ref[...]
        o_ref[0, pl.ds(r0, L), :] = (jax.nn.sigmoid(mo_ref[0, pl.ds(r0, L), :]) * y).astype(BF16)
        return carry

    lax.fori_loop(0, n_chunks, merge, 0)


def _mlstm(mqkv, gates_t, mo, g_mh, gones, esel, tril, triu, t_ctx):
    bsz, tt, _ = mqkv.shape
    L = ML_CHUNK
    nc = tt // L
    kern = functools.partial(_mlstm_kernel, n_ctx_chunks=t_ctx // L)
    per_b = lambda b: (b, 0, 0)
    const = lambda b: (0, 0)
    return pl.pallas_call(
        kern,
        grid=(bsz,),
        in_specs=[pl.BlockSpec((1, tt, 3 * ML_W), per_b),
                  pl.BlockSpec((1, nc, N_GATES, L), lambda b: (b, 0, 0, 0)),
                  pl.BlockSpec((1, tt, ML_W), per_b),
                  pl.BlockSpec((1, ML_W), const),
                  pl.BlockSpec((LANES, LANES), const),
                  pl.BlockSpec((2, N_GATES, ML_W), lambda b: (0, 0, 0)),
                  pl.BlockSpec((L, L), const),
                  pl.BlockSpec((L, L), const)],
        out_specs=pl.BlockSpec((1, tt, ML_W), per_b),
        out_shape=jax.ShapeDtypeStruct((bsz, tt, ML_W), BF16),
        scratch_shapes=[pltpu.VMEM((tt, ML_W), F32), pltpu.VMEM((tt, ML_W), F32),
                        pltpu.VMEM((2, ML_W, 2 * ML_W), F32), pltpu.VMEM((2, 8, ML_W), F32),
                        pltpu.VMEM((2, L, L), F32), pltpu.VMEM((ML_W, 2 * ML_W), F32)],
        compiler_params=_cparams("parallel"),
        name="mlstm",
    )(mqkv, gates_t, mo, g_mh, gones, esel, tril, triu)


def _cmlp_kernel(uv_ref, gv_ref, gones_ref, wsp_ref, bsp_ref, o_ref):
    n_chunks = uv_ref.shape[1] // CM_CHUNK
    lane = lax.broadcasted_iota(jnp.int32, (CM_CHUNK, CM_W), 1)
    for c in range(n_chunks):
        rows = slice(c * CM_CHUNK, (c + 1) * CM_CHUNK)
        u = jax.nn.gelu(uv_ref[0, rows, 0:CM_W])
        v = jax.nn.gelu(uv_ref[0, rows, CM_W:2 * CM_W])
        gms = _group_mean_sq(v, gones_ref[...])
        vb = ((v * lax.rsqrt(gms + EPS)) * gv_ref[...]).astype(BF16)
        z = jnp.zeros((CM_CHUNK, CM_W), F32)
        for g in range(CM_GROUPS):
            zg = jnp.dot(wsp_ref[g], vb, preferred_element_type=F32)
            z = jnp.where((lane // HEAD_DIM) == g, zg, z)
        o_ref[0, rows, :] = (u * (z + bsp_ref[...])).astype(BF16)


def _cmlp(cuv, g_v, gones, w_sp, b_sp_lanes):
    bsz, tt, _ = cuv.shape
    tm = ROW_TILE
    return pl.pallas_call(
        _cmlp_kernel,
        grid=(bsz, tt // tm),
        in_specs=[pl.BlockSpec((1, tm, 2 * CM_W), lambda b, t: (b, t, 0)),
                  pl.BlockSpec((1, CM_W), lambda b, t: (0, 0)),
                  pl.BlockSpec((LANES, LANES), lambda b, t: (0, 0)),
                  pl.BlockSpec((CM_GROUPS, CM_CHUNK, CM_CHUNK), lambda b, t: (0, 0, 0)),
                  pl.BlockSpec((CM_CHUNK, CM_W), lambda b, t: (0, 0))],
        out_specs=pl.BlockSpec((1, tm, CM_W), lambda b, t: (b, t, 0)),
        out_shape=jax.ShapeDtypeStruct((bsz, tt, CM_W), BF16),
        compiler_params=_cparams("parallel", "parallel"),
        name="cmlp",
    )(cuv, g_v, gones, w_sp, b_sp_lanes)


def _out_kernel(at_ref, ml_ref, cm_ref, x_ref, mod_ref, g2_ref, w_ref, xo_ref, h2_ref):
    d = D_MODEL
    mod = mod_ref[0]
    g1, sh2, s2 = mod[:, 2 * d:3 * d], mod[:, 3 * d:4 * d], mod[:, 4 * d:5 * d]
    mm = (jnp.dot(at_ref[0], w_ref[0:ATTN_W, :], preferred_element_type=F32)
          + jnp.dot(ml_ref[0], w_ref[ATTN_W:ATTN_W + ML_W, :], preferred_element_type=F32)
          + jnp.dot(cm_ref[0], w_ref[ATTN_W + ML_W:, :], preferred_element_type=F32))
    y = x_ref[0] + g1 * mm
    xo_ref[0] = y
    ms = jnp.mean(y * y, axis=-1, keepdims=True)
    h2_ref[0] = ((y * lax.rsqrt(ms + EPS)) * g2_ref[...] * (1 + s2) + sh2).astype(BF16)


def _out_proj(attn, ml, cm, x_all, mod_l, g2, w_out, n_ctx_tiles, skip_tiles):
    bsz, tt, d = x_all.shape
    tm = ROW_TILE
    t_out = tt - skip_tiles * tm
    row = lambda b, t: (b, t + skip_tiles, 0)
    out_row = lambda b, t: (b, t, 0)
    const = lambda b, t: (0, 0)
    return pl.pallas_call(
        _out_kernel,
        grid=(bsz, t_out // tm),
        in_specs=[pl.BlockSpec((1, tm, ATTN_W), row),
                  pl.BlockSpec((1, tm, ML_W), row),
                  pl.BlockSpec((1, tm, CM_W), row),
                  pl.BlockSpec((1, tm, d), row),
                  pl.BlockSpec((1, 1, 6 * d), lambda b, t: (jnp.where(t + skip_tiles < n_ctx_tiles, bsz, b), 0, 0)),
                  pl.BlockSpec((1, d), const),
                  pl.BlockSpec((d, d), const)],
        out_specs=[pl.BlockSpec((1, tm, d), out_row), pl.BlockSpec((1, tm, d), out_row)],
        out_shape=[jax.ShapeDtypeStruct((bsz, t_out, d), F32), jax.ShapeDtypeStruct((bsz, t_out, d), BF16)],
        compiler_params=_cparams("parallel", "parallel"),
        name="out_proj",
    )(attn, ml, cm, x_all, mod_l, g2, w_out)


def _ffn_kernel(h_ref, hp_ref, hn_ref, x_ref, mod_ref, wu_ref, cw_ref, wd_ref, o_ref, hcat_ref, act_ref, *,
                n_ctx_tiles):
    d = D_MODEL
    tm = h_ref.shape[1]
    t = pl.program_id(1)
    nt = pl.num_programs(1)
    has_prev = jnp.logical_and(t != 0, t != n_ctx_tiles)
    has_next = jnp.logical_and(t != n_ctx_tiles - 1, t != nt - 1)
    hrow = lax.broadcasted_iota(jnp.int32, (HALO, d), 0)
    before = pltpu.roll(hp_ref[0].astype(F32), 1, axis=0)
    after = pltpu.roll(hn_ref[0].astype(F32), 7, axis=0)
    halo = jnp.where(jnp.logical_and(hrow == 0, has_prev), before,
                     jnp.where(jnp.logical_and(hrow == 7, has_next), after, 0.0))
    hcat_ref[0:tm, :] = h_ref[0]
    hcat_ref[tm:tm + HALO, :] = halo.astype(BF16)
    hc = hcat_ref[...]
    sub = lax.broadcasted_iota(jnp.int32, (8, FF_TILE), 0)

    for f in range(D_FF // FF_TILE):
        halves = []
        for off in (f * FF_TILE, D_FF + f * FF_TILE):
            sl = slice(off, off + FF_TILE)
            a = jnp.dot(hc, wu_ref[:, sl], preferred_element_type=F32)
            cur = a[0:tm]
            edge = a[tm:tm + 8]
            dn = pltpu.roll(cur, 1, axis=0)
            up = pltpu.roll(cur, tm - 1, axis=0)
            prev = jnp.concatenate([jnp.where(sub == 0, edge, dn[0:8]), dn[8:]], axis=0)
            nxt = jnp.concatenate([up[:tm - 8], jnp.where(sub == 7, edge, up[tm - 8:])], axis=0)
            halves.append(prev * cw_ref[0:1, sl] + cur * cw_ref[1:2, sl] + nxt * cw_ref[2:3, sl] + cw_ref[3:4, sl])
        gate, val = halves
        act_ref[:, f * FF_TILE:(f + 1) * FF_TILE] = ((gate * jax.nn.sigmoid(gate)) * val).astype(BF16)
    acc = jnp.dot(act_ref[...], wd_ref[...], preferred_element_type=F32)
    g2 = mod_ref[0][:, 5 * d:6 * d]
    o_ref[0] = x_ref[0] + g2 * acc


def _conv_ffn(h2, x_mid, mod_l, w_up, cw, w_down, n_ctx_tiles):
    bsz, tt, d = x_mid.shape
    tm = ROW_TILE
    nt = tt // tm
    hb = tm // HALO
    n_halo_blocks = tt // HALO
    row = lambda b, t: (b, t, 0)
    const = lambda b, t: (0, 0)
    kern = functools.partial(_ffn_kernel, n_ctx_tiles=n_ctx_tiles)
    return pl.pallas_call(
        kern,
        grid=(bsz, nt),
        in_specs=[pl.BlockSpec((1, tm, d), row),
                  pl.BlockSpec((1, HALO, d), lambda b, t: (b, jnp.maximum(t * hb - 1, 0), 0)),
                  pl.BlockSpec((1, HALO, d), lambda b, t: (b, jnp.minimum((t + 1) * hb, n_halo_blocks - 1), 0)),
                  pl.BlockSpec((1, tm, d), row),
                  pl.BlockSpec((1, 1, 6 * d), lambda b, t: (jnp.where(t < n_ctx_tiles, bsz, b), 0, 0)),
                  pl.BlockSpec((d, 2 * D_FF), const),
                  pl.BlockSpec((8, 2 * D_FF), const),
                  pl.BlockSpec((D_FF, d), const)],
        out_specs=pl.BlockSpec((1, tm, d), row),
        out_shape=jax.ShapeDtypeStruct((bsz, tt, d), F32),
        scratch_shapes=[pltpu.VMEM((tm + HALO, d), BF16), pltpu.VMEM((tm, D_FF), BF16)],
        compiler_params=_cparams("parallel", "parallel"),
        name="conv_ffn",
    )(h2, h2, h2, x_mid, mod_l, w_up, cw, w_down)


def _rope_tables(s_len, t_ctx):
    nf = HEAD_DIM // 4
    rows = s_len // GRID_W
    pos = jnp.arange(s_len)
    row = jnp.repeat(jnp.arange(rows), GRID_W).astype(F32)
    colp = (pos % GRID_W).astype(F32)
    inv = ROPE_THETA ** (-jnp.arange(nf, dtype=F32) / nf)
    ang_r, ang_c = row[:, None] * inv[None], colp[:, None] * inv[None]
    cr, cc, sr, sc = jnp.cos(ang_r), jnp.cos(ang_c), jnp.sin(ang_r), jnp.sin(ang_c)
    z = jnp.zeros_like(sr)
    cos = jnp.concatenate([cr, cr, cc, cc], axis=1)
    sa = jnp.concatenate([-sr, z, -sc, z], axis=1)
    sb = jnp.concatenate([z, sr, z, sc], axis=1)
    pad = lambda a, fill: jnp.concatenate([jnp.full((t_ctx, HEAD_DIM), fill, F32), a], axis=0)
    two = lambda a: jnp.concatenate([a, a], axis=1)
    return two(pad(cos, 1.0)), two(pad(sa, 0.0)), two(pad(sb, 0.0))


def kernel(x, c, ctx, c_ctx, w_ada, b_ada, g_norm1, w_in, b_in, g_q, g_k, g_mh, g_v, w_sp, b_sp,
           w_out, g_norm2, w_up, conv_w, conv_b, w_down):
    bsz, s_len, d = x.shape
    t_ctx = ctx.shape[1]
    depth = w_ada.shape[0]
    tt = t_ctx + s_len
    assert d == D_MODEL and t_ctx % ML_CHUNK == 0 and s_len % ML_CHUNK == 0 and t_ctx % ROW_TILE == 0
    n_ctx_tiles = t_ctx // ROW_TILE

    o = ATTN_W + 2 * KV_W
    gate_off = o + 4 * ML_W
    cm_off = gate_off + N_GATES
    segs = ([(h * HEAD_DIM, (h + 1) * HEAD_DIM) for h in Q_HEAD_ORDER]
            + [(ATTN_W, gate_off), (cm_off, cm_off + 2 * CM_W), (gate_off, gate_off + N_GATES)])
    pad = N_PROJ - sum(e - s for s, e in segs)
    w_in_p = jnp.concatenate([w_in[:, :, s:e].astype(BF16) for s, e in segs]
                             + [jnp.zeros((depth, d, pad), BF16)], axis=2)
    b_in_p = jnp.concatenate([b_in[:, s:e] for s, e in segs] + [jnp.zeros((depth, pad), F32)], axis=1)[:, None, :]
    gqk = jnp.concatenate([jnp.tile(g_q, (1, N_Q_HEADS)) * (HEAD_DIM ** -0.5 * LOG2_E), jnp.tile(g_k, (1, N_KV_HEADS))],
                          axis=1)[:, None, :]
    w_out_p = jnp.concatenate([w_out[:, h * HEAD_DIM:(h + 1) * HEAD_DIM, :].astype(BF16) for h in Q_HEAD_ORDER]
                              + [w_out[:, ATTN_W:, :].astype(BF16)], axis=1)
    w_up_b = w_up.astype(BF16)
    w_down_b = w_down.astype(BF16)
    cw = jnp.concatenate([conv_w, conv_b[:, None, :], jnp.zeros((depth, 4, 2 * D_FF), F32)], axis=1)
    w_sp_b = w_sp.astype(BF16)
    b_sp_l = jnp.repeat(jnp.swapaxes(b_sp, 1, 2), HEAD_DIM, axis=2)
    blk = np.arange(LANES) // HEAD_DIM
    gones = jnp.asarray(blk[:, None] == blk[None, :], BF16)
    ii = np.arange(ML_CHUNK)
    tril = jnp.asarray(ii[None, :] <= ii[:, None], BF16)
    triu = jnp.asarray(ii[None, :] >= ii[:, None], BF16)
    gate_row = np.arange(N_GATES)[:, None]
    lane_head = (np.arange(ML_W) // HEAD_DIM)[None, :]
    esel = jnp.asarray(np.stack([gate_row == ML_HEADS + lane_head, gate_row == 3 * ML_HEADS + lane_head]), BF16)
    cos_t, sa_t, sb_t = _rope_tables(s_len, t_ctx)

    n_rows = -(-(bsz + 1) // 8) * 8
    cs = jnp.concatenate([c, c_ctx[None, :], jnp.zeros((n_rows - bsz - 1, d), F32)], axis=0)
    mod = _modulation(cs, w_ada, b_ada)

    x_all = jnp.concatenate([ctx, x], axis=1)
    for l in range(depth):
        mod_l = mod[l][:, None, :]
        qk, v, mqkv, mo, cuv, gts = _in_proj(x_all, mod_l, g_norm1[l][None], w_in_p[l], b_in_p[l], gqk[l], gones,
                                             cos_t, sa_t, sb_t, n_ctx_tiles)
        attn = _attention(qk, v, t_ctx)
        gates_t = gts[:, :, :N_GATES].reshape(bsz, tt // ML_CHUNK, ML_CHUNK, N_GATES).transpose(0, 1, 3, 2)
        ml = _mlstm(mqkv, gates_t, mo, g_mh[l][None], gones, esel, tril, triu, t_ctx)
        cm = _cmlp(cuv, g_v[l][None], gones, w_sp_b[l], b_sp_l[l])
        skip = n_ctx_tiles if l == depth - 1 else 0
        x_mid, h2 = _out_proj(attn, ml, cm, x_all, mod_l, g_norm2[l][None], w_out_p[l], n_ctx_tiles, skip)
        x_all = _conv_ffn(h2, x_mid, mod_l, w_up_b[l], cw[l], w_down_b[l], n_ctx_tiles - skip)
    return x_all
```

```python
import functools

import jax
import jax.numpy as jnp
import numpy as np
from jax import lax
from jax.experimental import pallas as pl
from jax.experimental.pallas import tpu as pltpu

F32 = jnp.float32
BF16 = jnp.bfloat16

D_MODEL = 1024
HEAD_DIM = 64
N_Q_HEADS = 8
N_KV_HEADS = 2
GRID_W = 64
ROPE_THETA = 10000.0
EPS = 1e-6
ATTN_W = N_Q_HEADS * HEAD_DIM
KV_W = N_KV_HEADS * HEAD_DIM
ML_HEADS = 4
ML_W = ML_HEADS * HEAD_DIM
CM_GROUPS = 4
CM_W = CM_GROUPS * HEAD_DIM
CM_CHUNK = 128
N_GATES = 4 * ML_HEADS
D_FF = 2816
LANES = 128

OFF_Q, OFF_K, OFF_V = 0, 512, 640
OFF_MQ, OFF_MO, OFF_CU, OFF_GT = 768, 1536, 1792, 2304
N_PROJ = 2432
QK_W = ATTN_W + KV_W

ROW_TILE = 256
ML_CHUNK = 256
FF_TILE = 256
HALO = 16
ATTN_CHUNK = 128
LOG2_E = 1.4426950408889634
VMEM_LIMIT = 56 * 1024 * 1024

Q_HEAD_ORDER = (0, 4, 1, 5, 2, 6, 3, 7)


def _cparams(*sem):
    return pltpu.CompilerParams(dimension_semantics=sem, vmem_limit_bytes=VMEM_LIMIT)


def _group_mean_sq(x, g_ones):
    x2 = x * x
    hi = x2.astype(BF16)
    lo = (x2 - hi.astype(F32)).astype(BF16)
    g2 = jnp.concatenate([g_ones, g_ones], axis=0)
    cols = []
    for j in range(x.shape[1] // LANES):
        sl = slice(j * LANES, (j + 1) * LANES)
        cols.append(jnp.dot(jnp.concatenate([hi[:, sl], lo[:, sl]], axis=1), g2, preferred_element_type=F32))
    s = cols[0] if len(cols) == 1 else jnp.concatenate(cols, axis=1)
    return s * (1.0 / HEAD_DIM)


def _mod_kernel(c_ref, w_ref, b_ref, o_ref):
    c = c_ref[...]
    sc = c * jax.nn.sigmoid(c)
    o_ref[0] = jnp.dot(sc, w_ref[0], preferred_element_type=F32,
                       precision=lax.Precision.HIGHEST) + b_ref[0]


def _modulation(cs, w_ada, b_ada):
    depth, d, d6 = w_ada.shape
    r = cs.shape[0]
    nj = d6 // d
    return pl.pallas_call(
        _mod_kernel,
        grid=(depth, nj),
        in_specs=[pl.BlockSpec((r, d), lambda l, j: (0, 0)),
                  pl.BlockSpec((1, d, d), lambda l, j: (l, 0, j)),
                  pl.BlockSpec((1, 1, d), lambda l, j: (l, 0, j))],
        out_specs=pl.BlockSpec((1, r, d), lambda l, j: (l, 0, j)),
        out_shape=jax.ShapeDtypeStruct((depth, r, d6), F32),
        compiler_params=_cparams("arbitrary", "arbitrary"),
        name="adaln_mod",
    )(cs, w_ada, b_ada.reshape(depth, 1, d6))


def _in_kernel(x_ref, mod_ref, g1_ref, w_ref, b_ref, gqk_ref, gones_ref, cos_ref, sa_ref, sb_ref,
               qk_ref, v_ref, mqkv_ref, mo_ref, cuv_ref, gt_ref):
    d = D_MODEL
    x = x_ref[0]
    mod = mod_ref[0]
    sh1, s1 = mod[:, 0:d], mod[:, d:2 * d]
    ms = jnp.mean(x * x, axis=-1, keepdims=True)
    h = (x * lax.rsqrt(ms + EPS)) * g1_ref[...] * (1 + s1) + sh1
    hb = h.astype(BF16)
    p = jnp.concatenate([jnp.dot(hb, w_ref[:, 0:OFF_MQ], preferred_element_type=F32),
                         jnp.dot(hb, w_ref[:, OFF_MQ:], preferred_element_type=F32)], axis=1) + b_ref[...]

    qk = p[:, OFF_Q:OFF_Q + QK_W]
    gms = _group_mean_sq(qk, gones_ref[...])
    qn = (qk * lax.rsqrt(gms + EPS)) * gqk_ref[...]
    cos, sa, sb = cos_ref[...], sa_ref[...], sb_ref[...]
    for j in range(QK_W // LANES):
        xb = qn[:, j * LANES:(j + 1) * LANES]
        up = pltpu.roll(xb, LANES - 16, axis=1)
        dn = pltpu.roll(xb, 16, axis=1)
        qk_ref[0, :, j * LANES:(j + 1) * LANES] = (xb * cos + up * sa + dn * sb).astype(BF16)

    v_ref[0] = p[:, OFF_V:OFF_V + KV_W].T.astype(BF16)
    mqkv_ref[0, :, 0:ML_W] = p[:, OFF_MQ:OFF_MQ + ML_W].astype(BF16)
    mqkv_ref[0, :, ML_W:2 * ML_W] = (p[:, OFF_MQ + ML_W:OFF_MQ + 2 * ML_W] * (HEAD_DIM ** -0.5)).astype(BF16)
    mqkv_ref[0, :, 2 * ML_W:3 * ML_W] = p[:, OFF_MQ + 2 * ML_W:OFF_MQ + 3 * ML_W].astype(BF16)
    mo_ref[0] = p[:, OFF_MO:OFF_MO + ML_W]
    cuv_ref[0] = p[:, OFF_CU:OFF_CU + 2 * CM_W]
    gt_ref[0] = p[:, OFF_GT:OFF_GT + LANES]


def _in_proj(x_all, mod_l, g1, w_in, b_in, gqk, gones, cos_t, sa_t, sb_t, n_ctx_tiles):
    bsz, tt, d = x_all.shape
    tm = ROW_TILE
    nt = tt // tm
    row = lambda b, t: (b, t, 0)
    const = lambda b, t: (0, 0)
    tab = lambda b, t: (t, 0)
    outs = [(QK_W, BF16), (KV_W, BF16), (3 * ML_W, BF16), (ML_W, F32), (2 * CM_W, F32), (LANES, F32)]
    return pl.pallas_call(
        _in_kernel,
        grid=(bsz, nt),
        in_specs=[pl.BlockSpec((1, tm, d), row),
                  pl.BlockSpec((1, 1, 6 * d), lambda b, t: (jnp.where(t < n_ctx_tiles, bsz, b), 0, 0)),
                  pl.BlockSpec((1, d), const),
                  pl.BlockSpec((d, N_PROJ), const),
                  pl.BlockSpec((1, N_PROJ), const),
                  pl.BlockSpec((1, QK_W), const),
                  pl.BlockSpec((LANES, LANES), const),
                  pl.BlockSpec((tm, LANES), tab),
                  pl.BlockSpec((tm, LANES), tab),
                  pl.BlockSpec((tm, LANES), tab)],
        out_specs=[pl.BlockSpec((1, KV_W, tm), lambda b, t: (b, 0, t)) if i == 1 else pl.BlockSpec((1, tm, w), row)
                   for i, (w, _) in enumerate(outs)],
        out_shape=[jax.ShapeDtypeStruct((bsz, KV_W, tt) if i == 1 else (bsz, tt, w), dt)
                   for i, (w, dt) in enumerate(outs)],
        compiler_params=_cparams("parallel", "parallel"),
        name="in_proj",
    )(x_all, mod_l, g1, w_in, b_in, gqk, gones, cos_t, sa_t, sb_t)


def _attn_kernel(q_ref, k_ref, vt_ref, o_ref, s_ref, e_ref, *, n_ctx_tiles, t_ctx):
    t = pl.program_id(1)
    tq = q_ref.shape[1]
    low = lax.broadcasted_iota(jnp.int32, (tq, LANES), 1) < HEAD_DIM
    n_pairs = ATTN_W // LANES
    groups = [(kvh, pp) for pp in range(n_pairs // 2) for kvh in range(2)]

    def attend(kv_len):
        def scores(g):
            k = k_ref[0, 0:kv_len, :]
            kvh, pp = groups[g]
            sel = low if kvh == 0 else jnp.logical_not(low)
            q2 = jnp.concatenate(
                [jnp.where(sel, q_ref[0, :, p * LANES:(p + 1) * LANES], jnp.zeros((tq, LANES), BF16))
                 for p in (2 * pp, 2 * pp + 1)], axis=0)
            s_ref[g % 2, 0:kv_len, :] = lax.dot_general(k, q2, (((1,), (1,)), ((), ())),
                                                        preferred_element_type=F32)

        scores(0)
        outs = {}
        for g, (kvh, pp) in enumerate(groups):
            if g + 1 < len(groups):
                scores(g + 1)
            chunks = [slice(c, c + ATTN_CHUNK) for c in range(0, kv_len, ATTN_CHUNK)]
            fold = lambda a: a.reshape(ATTN_CHUNK // 8, 8, 2 * tq)
            m8 = None
            for rows in chunks:
                mc = jnp.max(fold(s_ref[g % 2, rows, :]), axis=0)
                m8 = mc if m8 is None else jnp.maximum(m8, mc)
            m = jnp.max(m8, axis=0, keepdims=True)
            for rows in chunks:
                e_ref[g % 2, rows, :] = jnp.exp2(s_ref[g % 2, rows, :] - m).astype(BF16)
            vt = vt_ref[0, kvh * HEAD_DIM:(kvh + 1) * HEAD_DIM, 0:kv_len]
            vt1 = jnp.concatenate([vt, jnp.ones((16, kv_len), BF16)], axis=0)
            ov = jnp.dot(vt1, e_ref[g % 2, 0:kv_len, :], preferred_element_type=F32)
            outs[kvh] = ov[0:HEAD_DIM] / ov[HEAD_DIM:HEAD_DIM + 1]
            if kvh == 1:
                for j, p in enumerate((2 * pp, 2 * pp + 1)):
                    both = jnp.concatenate([outs[0][:, j * tq:(j + 1) * tq], outs[1][:, j * tq:(j + 1) * tq]], axis=0)
                    o_ref[0, :, p * LANES:(p + 1) * LANES] = both.T.astype(BF16)

    @pl.when(t < n_ctx_tiles)
    def _():
        attend(t_ctx)

    @pl.when(t >= n_ctx_tiles)
    def _():
        attend(k_ref.shape[1])


def _attention(qk, vt, t_ctx):
    bsz, tt, _ = qk.shape
    tq = ROW_TILE
    kern = functools.partial(_attn_kernel, n_ctx_tiles=t_ctx // tq, t_ctx=t_ctx)
    return pl.pallas_call(
        kern,
        grid=(bsz, tt // tq),
        in_specs=[pl.BlockSpec((1, tq, ATTN_W), lambda b, t: (b, t, 0)),
                  pl.BlockSpec((1, tt, KV_W), lambda b, t: (b, 0, ATTN_W // KV_W)),
                  pl.BlockSpec((1, KV_W, tt), lambda b, t: (b, 0, 0))],
        out_specs=pl.BlockSpec((1, tq, ATTN_W), lambda b, t: (b, t, 0)),
        out_shape=jax.ShapeDtypeStruct((bsz, tt, ATTN_W), BF16),
        scratch_shapes=[pltpu.VMEM((2, tt, 2 * tq), F32), pltpu.VMEM((2, tt, 2 * tq), BF16)],
        compiler_params=_cparams("parallel", "parallel"),
        name="attention",
    )(qk, qk, vt)


def _log_sigmoid(x):
    return jnp.minimum(x, 0.0) - jnp.log1p(jnp.exp(-jnp.abs(x)))


def _split3(a):
    hi = a.astype(BF16)
    r1 = a - hi.astype(F32)
    mid = r1.astype(BF16)
    lo = (r1 - mid.astype(F32)).astype(BF16)
    return hi, mid, lo


def _mlstm_kernel(qkv_ref, gt_ref, mo_ref, gmh_ref, gones_ref, esel_ref, tril_ref, triu_ref,
                  o_ref, hf_ref, hb_ref, cn_ref, m_ref, neg_ref, cmask_ref, *, n_ctx_chunks):
    L = ML_CHUNK
    W = ML_W
    n_chunks = qkv_ref.shape[1] // L

    row = lax.broadcasted_iota(jnp.int32, (L, L), 0)
    col = lax.broadcasted_iota(jnp.int32, (L, L), 1)
    neg_ref[0] = jnp.where(col <= row, 0.0, -jnp.inf)
    neg_ref[1] = jnp.where(col >= row, 0.0, -jnp.inf)
    blk_r = lax.broadcasted_iota(jnp.int32, (W, 2 * W), 0) // HEAD_DIM
    blk_c = (lax.broadcasted_iota(jnp.int32, (W, 2 * W), 1) % W) // HEAD_DIM
    cmask_ref[...] = (blk_r == blk_c).astype(F32)
    low = lax.broadcasted_iota(jnp.int32, (L, LANES), 1) < HEAD_DIM
    ones_blk = jnp.ones((L, W), BF16)
    cn_ref[...] = jnp.zeros_like(cn_ref)
    m_ref[...] = jnp.zeros_like(m_ref)

    def step(rev, c):
        r0 = pl.multiple_of(c * L, L)
        q = qkv_ref[0, pl.ds(r0, L), 0:W]
        k = qkv_ref[0, pl.ds(r0, L), W:2 * W]
        v = qkv_ref[0, pl.ds(r0, L), 2 * W:3 * W]
        gt = gt_ref[0, c]
        goff = 2 * ML_HEADS if rev else 0
        tri_r = tril_ref[...] if rev else triu_ref[...]
        last = 0 if rev else L - 1

        brows = sum(jnp.dot(part, tri_r, preferred_element_type=F32) for part in _split3(_log_sigmoid(gt)))
        bexp = sum(lax.dot_general(part, esel_ref[rev], (((0,), (0,)), ((), ())), preferred_element_type=F32)
                   for part in _split3(brows))
        r_rows = gt[goff:goff + ML_HEADS] - brows[goff + ML_HEADS:goff + 2 * ML_HEADS]

        cn = cn_ref[rev]
        m_prev = m_ref[rev][0:1, :]
        qc = jnp.dot(q, cn.astype(BF16), preferred_element_type=F32)
        kt = k.astype(F32).T
        kt_b = kt.astype(BF16)

        mus, asums, avs, gsr = [], [], [], []
        for h in range(ML_HEADS):
            cols = slice((h // 2) * LANES, (h // 2 + 1) * LANES)
            sel = low if h % 2 == 0 else jnp.logical_not(low)
            qh = jnp.where(sel, q[:, cols], jnp.zeros((L, LANES), BF16))
            s = jnp.dot(qh, kt_b[cols, :], preferred_element_type=F32)
            r = r_rows[h:h + 1, :]
            rm = r + neg_ref[rev]
            mu = jnp.maximum(jnp.max(rm, axis=-1, keepdims=True), m_prev[:, h * HEAD_DIM:h * HEAD_DIM + 1])
            a = jnp.exp(rm - mu) * s
            asums.append(jnp.sum(a, axis=-1, keepdims=True))
            avs.append(jnp.dot(a.astype(BF16), v[:, cols], preferred_element_type=F32))
            mus.append(mu)
            gsr.append(jnp.broadcast_to(jnp.exp(r - mu[last:last + 1, :]), (HEAD_DIM, L)))

        pair = lambda xs: jnp.concatenate([jnp.where(low, xs[0], xs[1]), jnp.where(low, xs[2], xs[3])], axis=1)
        mu_x, asum_x, av_x = pair(mus), pair(asums), pair(avs)
        w_inter = jnp.exp(m_prev - mu_x)
        m_t = bexp + mu_x
        num = w_inter * qc[:, 0:W] + av_x
        den = w_inter * qc[:, W:2 * W] + asum_x
        dst = hb_ref if rev else hf_ref
        dst[pl.ds(r0, L), :] = num / jnp.maximum(jnp.abs(den), jnp.exp(-m_t))

        mu_last = mu_x[last:last + 1, :]
        wc = jnp.exp(m_prev - mu_last)
        gkt = (kt * jnp.concatenate(gsr, axis=0)).astype(BF16)
        upd = jnp.dot(gkt, jnp.concatenate([v, ones_blk], axis=1), preferred_element_type=F32)
        cn_ref[rev] = jnp.concatenate([wc, wc], axis=1) * cn + upd * cmask_ref[...]
        m_ref[rev] = jnp.broadcast_to(bexp[last:last + 1, :] + mu_last, (8, W))

    def body(j, carry):
        step(0, j)
        cb = jnp.where(j < n_ctx_chunks, n_ctx_chunks - 1 - j, n_chunks - 1 - (j - n_ctx_chunks))
        step(1, cb)
        return carry

    lax.fori_loop(0, n_chunks, body, 0)

    def merge(c, carry):
        r0 = pl.multiple_of(c * L, L)
        hs = hf_ref[pl.ds(r0, L), :] + hb_ref[pl.ds(r0, L), :]
        gms = _group_mean_sq(hs, gones_ref[...])
        y = (hs * lax.rsqrt(gms + EPS)) * gmh_ref[...]
        o_ref[0, pl.ds(r0, L), :] = (jax.nn.sigmoid(mo_ref[0, pl.ds(r0, L), :]) * y).astype(BF16)
        return carry

    lax.fori_loop(0, n_chunks, merge, 0)


def _mlstm(mqkv, gates_t, mo, g_mh, gones, esel, tril, triu, t_ctx):
    bsz, tt, _ = mqkv.shape
    L = ML_CHUNK
    nc = tt // L
    kern = functools.partial(_mlstm_kernel, n_ctx_chunks=t_ctx // L)
    per_b = lambda b: (b, 0, 0)
    const = lambda b: (0, 0)
    return pl.pallas_call(
        kern,
        grid=(bsz,),
        in_specs=[pl.BlockSpec((1, tt, 3 * ML_W), per_b),
                  pl.BlockSpec((1, nc, N_GATES, L), lambda b: (b, 0, 0, 0)),
                  pl.BlockSpec((1, tt, ML_W), per_b),
                  pl.BlockSpec((1, ML_W), const),
                  pl.BlockSpec((LANES, LANES), const),
                  pl.BlockSpec((2, N_GATES, ML_W), lambda b: (0, 0, 0)),
                  pl.BlockSpec((L, L), const),
                  pl.BlockSpec((L, L), const)],
        out_specs=pl.BlockSpec((1, tt, ML_W), per_b),
        out_shape=jax.ShapeDtypeStruct((bsz, tt, ML_W), BF16),
        scratch_shapes=[pltpu.VMEM((tt, ML_W), F32), pltpu.VMEM((tt, ML_W), F32),
                        pltpu.VMEM((2, ML_W, 2 * ML_W), F32), pltpu.VMEM((2, 8, ML_W), F32),
                        pltpu.VMEM((2, L, L), F32), pltpu.VMEM((ML_W, 2 * ML_W), F32)],
        compiler_params=_cparams("parallel"),
        name="mlstm",
    )(mqkv, gates_t, mo, g_mh, gones, esel, tril, triu)


def _chunk_mlp(uv, g_v, g_ones, wsp_ref, b_sp):
    lane = lax.broadcasted_iota(jnp.int32, (CM_CHUNK, CM_W), 1)
    u = jax.nn.gelu(uv[:, 0:CM_W])
    v = jax.nn.gelu(uv[:, CM_W:2 * CM_W])
    gms = _group_mean_sq(v, g_ones)
    vb = ((v * lax.rsqrt(gms + EPS)) * g_v).astype(BF16)
    z = jnp.zeros((CM_CHUNK, CM_W), F32)
    for g in range(CM_GROUPS):
        zg = jnp.dot(wsp_ref[g], vb, preferred_element_type=F32)
        z = jnp.where((lane // HEAD_DIM) == g, zg, z)
    return (u * (z + b_sp)).astype(BF16)


def _out_kernel(at_ref, ml_ref, uv_ref, x_ref, mod_ref, g2_ref, w_ref, gv_ref, gones_ref, wsp_ref, bsp_ref,
                xo_ref, h2_ref):
    d = D_MODEL
    tm = x_ref.shape[1]
    mod = mod_ref[0]
    g1, sh2, s2 = mod[:, 2 * d:3 * d], mod[:, 3 * d:4 * d], mod[:, 4 * d:5 * d]
    cm = jnp.concatenate([_chunk_mlp(uv_ref[0, c:c + CM_CHUNK, :], gv_ref[...], gones_ref[...], wsp_ref, bsp_ref[...])
                          for c in range(0, tm, CM_CHUNK)], axis=0)
    mm = (jnp.dot(at_ref[0], w_ref[0:ATTN_W, :], preferred_element_type=F32)
          + jnp.dot(ml_ref[0], w_ref[ATTN_W:ATTN_W + ML_W, :], preferred_element_type=F32)
          + jnp.dot(cm, w_ref[ATTN_W + ML_W:, :], preferred_element_type=F32))
    y = x_ref[0] + g1 * mm
    xo_ref[0] = y
    ms = jnp.mean(y * y, axis=-1, keepdims=True)
    h2_ref[0] = ((y * lax.rsqrt(ms + EPS)) * g2_ref[...] * (1 + s2) + sh2).astype(BF16)


def _out_proj(attn, ml, cuv, x_all, mod_l, g2, w_out, g_v, gones, w_sp, b_sp_lanes, n_ctx_tiles, skip_tiles):
    bsz, tt, d = x_all.shape
    tm = ROW_TILE
    t_out = tt - skip_tiles * tm
    row = lambda b, t: (b, t + skip_tiles, 0)
    out_row = lambda b, t: (b, t, 0)
    const = lambda b, t: (0, 0)
    return pl.pallas_call(
        _out_kernel,
        grid=(bsz, t_out // tm),
        in_specs=[pl.BlockSpec((1, tm, ATTN_W), row),
                  pl.BlockSpec((1, tm, ML_W), row),
                  pl.BlockSpec((1, tm, 2 * CM_W), row),
                  pl.BlockSpec((1, tm, d), row),
                  pl.BlockSpec((1, 1, 6 * d), lambda b, t: (jnp.where(t + skip_tiles < n_ctx_tiles, bsz, b), 0, 0)),
                  pl.BlockSpec((1, d), const),
                  pl.BlockSpec((d, d), const),
                  pl.BlockSpec((1, CM_W), const),
                  pl.BlockSpec((LANES, LANES), const),
                  pl.BlockSpec((CM_GROUPS, CM_CHUNK, CM_CHUNK), lambda b, t: (0, 0, 0)),
                  pl.BlockSpec((CM_CHUNK, CM_W), const)],
        out_specs=[pl.BlockSpec((1, tm, d), out_row), pl.BlockSpec((1, tm, d), out_row)],
        out_shape=[jax.ShapeDtypeStruct((bsz, t_out, d), F32), jax.ShapeDtypeStruct((bsz, t_out, d), BF16)],
        compiler_params=_cparams("parallel", "parallel"),
        name="out_proj",
    )(attn, ml, cuv, x_all, mod_l, g2, w_out, g_v, gones, w_sp, b_sp_lanes)


def _ffn_kernel(h_ref, hp_ref, hn_ref, x_ref, mod_ref, wu_ref, cw_ref, wd_ref, o_ref, hcat_ref, act_ref, *,
                n_ctx_tiles):
    d = D_MODEL
    tm = h_ref.shape[1]
    t = pl.program_id(1)
    nt = pl.num_programs(1)
    has_prev = jnp.logical_and(t != 0, t != n_ctx_tiles)
    has_next = jnp.logical_and(t != n_ctx_tiles - 1, t != nt - 1)
    hrow = lax.broadcasted_iota(jnp.int32, (HALO, d), 0)
    before = pltpu.roll(hp_ref[0].astype(F32), 1, axis=0)
    after = pltpu.roll(hn_ref[0].astype(F32), 7, axis=0)
    halo = jnp.where(jnp.logical_and(hrow == 0, has_prev), before,
                     jnp.where(jnp.logical_and(hrow == 7, has_next), after, 0.0))
    hcat_ref[0:tm, :] = h_ref[0]
    hcat_ref[tm:tm + HALO, :] = halo.astype(BF16)
    hc = hcat_ref[...]
    sub = lax.broadcasted_iota(jnp.int32, (8, FF_TILE), 0)

    for f in range(D_FF // FF_TILE):
        halves = []
        for off in (f * FF_TILE, D_FF + f * FF_TILE):
            sl = slice(off, off + FF_TILE)
            a = jnp.dot(hc, wu_ref[:, sl], preferred_element_type=F32)
            cur = a[0:tm]
            edge = a[tm:tm + 8]
            dn = pltpu.roll(cur, 1, axis=0)
            up = pltpu.roll(cur, tm - 1, axis=0)
            prev = jnp.concatenate([jnp.where(sub == 0, edge, dn[0:8]), dn[8:]], axis=0)
            nxt = jnp.concatenate([up[:tm - 8], jnp.where(sub == 7, edge, up[tm - 8:])], axis=0)
            halves.append(prev * cw_ref[0:1, sl] + cur * cw_ref[1:2, sl] + nxt * cw_ref[2:3, sl] + cw_ref[3:4, sl])
        gate, val = halves
        act_ref[:, f * FF_TILE:(f + 1) * FF_TILE] = ((gate * jax.nn.sigmoid(gate)) * val).astype(BF16)
    acc = jnp.dot(act_ref[...], wd_ref[...], preferred_element_type=F32)
    g2 = mod_ref[0][:, 5 * d:6 * d]
    o_ref[0] = x_ref[0] + g2 * acc


def _conv_ffn(h2, x_mid, mod_l, w_up, cw, w_down, n_ctx_tiles):
    bsz, tt, d = x_mid.shape
    tm = ROW_TILE
    nt = tt // tm
    hb = tm // HALO
    n_halo_blocks = tt // HALO
    row = lambda b, t: (b, t, 0)
    const = lambda b, t: (0, 0)
    kern = functools.partial(_ffn_kernel, n_ctx_tiles=n_ctx_tiles)
    return pl.pallas_call(
        kern,
        grid=(bsz, nt),
        in_specs=[pl.BlockSpec((1, tm, d), row),
                  pl.BlockSpec((1, HALO, d), lambda b, t: (b, jnp.maximum(t * hb - 1, 0), 0)),
                  pl.BlockSpec((1, HALO, d), lambda b, t: (b, jnp.minimum((t + 1) * hb, n_halo_blocks - 1), 0)),
                  pl.BlockSpec((1, tm, d), row),
                  pl.BlockSpec((1, 1, 6 * d), lambda b, t: (jnp.where(t < n_ctx_tiles, bsz, b), 0, 0)),
                  pl.BlockSpec((d, 2 * D_FF), const),
                  pl.BlockSpec((8, 2 * D_FF), const),
                  pl.BlockSpec((D_FF, d), const)],
        out_specs=pl.BlockSpec((1, tm, d), row),
        out_shape=jax.ShapeDtypeStruct((bsz, tt, d), F32),
        scratch_shapes=[pltpu.VMEM((tm + HALO, d), BF16), pltpu.VMEM((tm, D_FF), BF16)],
        compiler_params=_cparams("parallel", "parallel"),
        name="conv_ffn",
    )(h2, h2, h2, x_mid, mod_l, w_up, cw, w_down)


def _rope_tables(s_len, t_ctx):
    nf = HEAD_DIM // 4
    rows = s_len // GRID_W
    pos = jnp.arange(s_len)
    row = jnp.repeat(jnp.arange(rows), GRID_W).astype(F32)
    colp = (pos % GRID_W).astype(F32)
    inv = ROPE_THETA ** (-jnp.arange(nf, dtype=F32) / nf)
    ang_r, ang_c = row[:, None] * inv[None], colp[:, None] * inv[None]
    cr, cc, sr, sc = jnp.cos(ang_r), jnp.cos(ang_c), jnp.sin(ang_r), jnp.sin(ang_c)
    z = jnp.zeros_like(sr)
    cos = jnp.concatenate([cr, cr, cc, cc], axis=1)
    sa = jnp.concatenate([-sr, z, -sc, z], axis=1)
    sb = jnp.concatenate([z, sr, z, sc], axis=1)
    pad = lambda a, fill: jnp.concatenate([jnp.full((t_ctx, HEAD_DIM), fill, F32), a], axis=0)
    two = lambda a: jnp.concatenate([a, a], axis=1)
    return two(pad(cos, 1.0)), two(pad(sa, 0.0)), two(pad(sb, 0.0))


def kernel(x, c, ctx, c_ctx, w_ada, b_ada, g_norm1, w_in, b_in, g_q, g_k, g_mh, g_v, w_sp, b_sp,
           w_out, g_norm2, w_up, conv_w, conv_b, w_down):
    bsz, s_len, d = x.shape
    t_ctx = ctx.shape[1]
    depth = w_ada.shape[0]
    tt = t_ctx + s_len
    assert d == D_MODEL and t_ctx % ML_CHUNK == 0 and s_len % ML_CHUNK == 0 and t_ctx % ROW_TILE == 0
    n_ctx_tiles = t_ctx // ROW_TILE

    o = ATTN_W + 2 * KV_W
    gate_off = o + 4 * ML_W
    cm_off = gate_off + N_GATES
    segs = ([(h * HEAD_DIM, (h + 1) * HEAD_DIM) for h in Q_HEAD_ORDER]
            + [(ATTN_W, gate_off), (cm_off, cm_off + 2 * CM_W), (gate_off, gate_off + N_GATES)])
    pad = N_PROJ - sum(e - s for s, e in segs)
    w_in_p = jnp.concatenate([w_in[:, :, s:e].astype(BF16) for s, e in segs]
                             + [jnp.zeros((depth, d, pad), BF16)], axis=2)
    b_in_p = jnp.concatenate([b_in[:, s:e] for s, e in segs] + [jnp.zeros((depth, pad), F32)], axis=1)[:, None, :]
    gqk = jnp.concatenate([jnp.tile(g_q, (1, N_Q_HEADS)) * (HEAD_DIM ** -0.5 * LOG2_E), jnp.tile(g_k, (1, N_KV_HEADS))],
                          axis=1)[:, None, :]
    w_out_p = jnp.concatenate([w_out[:, h * HEAD_DIM:(h + 1) * HEAD_DIM, :].astype(BF16) for h in Q_HEAD_ORDER]
                              + [w_out[:, ATTN_W:, :].astype(BF16)], axis=1)
    w_up_b = w_up.astype(BF16)
    w_down_b = w_down.astype(BF16)
    cw = jnp.concatenate([conv_w, conv_b[:, None, :], jnp.zeros((depth, 4, 2 * D_FF), F32)], axis=1)
    w_sp_b = w_sp.astype(BF16)
    b_sp_l = jnp.repeat(jnp.swapaxes(b_sp, 1, 2), HEAD_DIM, axis=2)
    blk = np.arange(LANES) // HEAD_DIM
    gones = jnp.asarray(blk[:, None] == blk[None, :], BF16)
    ii = np.arange(ML_CHUNK)
    tril = jnp.asarray(ii[None, :] <= ii[:, None], BF16)
    triu = jnp.asarray(ii[None, :] >= ii[:, None], BF16)
    gate_row = np.arange(N_GATES)[:, None]
    lane_head = (np.arange(ML_W) // HEAD_DIM)[None, :]
    esel = jnp.asarray(np.stack([gate_row == ML_HEADS + lane_head, gate_row == 3 * ML_HEADS + lane_head]), BF16)
    cos_t, sa_t, sb_t = _rope_tables(s_len, t_ctx)

    n_rows = -(-(bsz + 1) // 8) * 8
    cs = jnp.concatenate([c, c_ctx[None, :], jnp.zeros((n_rows - bsz - 1, d), F32)], axis=0)
    mod = _modulation(cs, w_ada, b_ada)

    x_all = jnp.concatenate([ctx, x], axis=1)
    for l in range(depth):
        mod_l = mod[l][:, None, :]
        qk, v, mqkv, mo, cuv, gts = _in_proj(x_all, mod_l, g_norm1[l][None], w_in_p[l], b_in_p[l], gqk[l], gones,
                                             cos_t, sa_t, sb_t, n_ctx_tiles)
        attn = _attention(qk, v, t_ctx)
        gates_t = gts[:, :, :N_GATES].reshape(bsz, tt // ML_CHUNK, ML_CHUNK, N_GATES).transpose(0, 1, 3, 2)
        ml = _mlstm(mqkv, gates_t, mo, g_mh[l][None], gones, esel, tril, triu, t_ctx)
        skip = n_ctx_tiles if l == depth - 1 else 0
        x_mid, h2 = _out_proj(attn, ml, cuv, x_all, mod_l, g_norm2[l][None], w_out_p[l], g_v[l][None], gones,
                              w_sp_b[l], b_sp_l[l], n_ctx_tiles, skip)
        x_all = _conv_ffn(h2, x_mid, mod_l, w_up_b[l], cw[l], w_down_b[l], n_ctx_tiles - skip)
    return x_all
```

```python
import functools

import jax
import jax.numpy as jnp
import numpy as np
from jax import lax
from jax.experimental import pallas as pl
from jax.experimental.pallas import tpu as pltpu

F32 = jnp.float32
BF16 = jnp.bfloat16

D_MODEL = 1024
HEAD_DIM = 64
N_Q_HEADS = 8
N_KV_HEADS = 2
GRID_W = 64
ROPE_THETA = 10000.0
EPS = 1e-6
ATTN_W = N_Q_HEADS * HEAD_DIM
KV_W = N_KV_HEADS * HEAD_DIM
ML_HEADS = 4
ML_W = ML_HEADS * HEAD_DIM
CM_GROUPS = 4
CM_W = CM_GROUPS * HEAD_DIM
CM_CHUNK = 128
N_GATES = 4 * ML_HEADS
D_FF = 2816
LANES = 128

OFF_Q, OFF_K, OFF_V = 0, 512, 640
OFF_MQ, OFF_MO, OFF_CU, OFF_GT = 768, 1536, 1792, 2304
N_PROJ = 2432
QK_W = ATTN_W + KV_W

ROW_TILE = 256
ML_CHUNK = 256
FF_TILE = 256
HALO = 16
ATTN_CHUNK = 128
LOG2_E = 1.4426950408889634
VMEM_LIMIT = 56 * 1024 * 1024

Q_HEAD_ORDER = (0, 4, 1, 5, 2, 6, 3, 7)


def _cparams(*sem):
    return pltpu.CompilerParams(dimension_semantics=sem, vmem_limit_bytes=VMEM_LIMIT)


def _group_mean_sq(x, g_ones):
    x2 = x * x
    hi = x2.astype(BF16)
    lo = (x2 - hi.astype(F32)).astype(BF16)
    g2 = jnp.concatenate([g_ones, g_ones], axis=0)
    cols = []
    for j in range(x.shape[1] // LANES):
        sl = slice(j * LANES, (j + 1) * LANES)
        cols.append(jnp.dot(jnp.concatenate([hi[:, sl], lo[:, sl]], axis=1), g2, preferred_element_type=F32))
    s = cols[0] if len(cols) == 1 else jnp.concatenate(cols, axis=1)
    return s * (1.0 / HEAD_DIM)


def _mod_kernel(c_ref, w_ref, b_ref, o_ref):
    c = c_ref[...]
    sc = c * jax.nn.sigmoid(c)
    o_ref[0] = jnp.dot(sc, w_ref[0], preferred_element_type=F32,
                       precision=lax.Precision.HIGHEST) + b_ref[0]


def _modulation(cs, w_ada, b_ada):
    depth, d, d6 = w_ada.shape
    r = cs.shape[0]
    nj = d6 // d
    return pl.pallas_call(
        _mod_kernel,
        grid=(depth, nj),
        in_specs=[pl.BlockSpec((r, d), lambda l, j: (0, 0)),
                  pl.BlockSpec((1, d, d), lambda l, j: (l, 0, j)),
                  pl.BlockSpec((1, 1, d), lambda l, j: (l, 0, j))],
        out_specs=pl.BlockSpec((1, r, d), lambda l, j: (l, 0, j)),
        out_shape=jax.ShapeDtypeStruct((depth, r, d6), F32),
        compiler_params=_cparams("arbitrary", "arbitrary"),
        name="adaln_mod",
    )(cs, w_ada, b_ada.reshape(depth, 1, d6))


def _in_kernel(x_ref, mod_ref, g1_ref, w_ref, b_ref, gqk_ref, gones_ref, cos_ref, sa_ref, sb_ref,
               qk_ref, v_ref, mqkv_ref, mo_ref, cuv_ref, gt_ref):
    d = D_MODEL
    x = x_ref[0]
    mod = mod_ref[0]
    sh1, s1 = mod[:, 0:d], mod[:, d:2 * d]
    ms = jnp.mean(x * x, axis=-1, keepdims=True)
    h = (x * lax.rsqrt(ms + EPS)) * g1_ref[...] * (1 + s1) + sh1
    hb = h.astype(BF16)
    p = jnp.concatenate([jnp.dot(hb, w_ref[:, 0:OFF_MQ], preferred_element_type=F32),
                         jnp.dot(hb, w_ref[:, OFF_MQ:], preferred_element_type=F32)], axis=1) + b_ref[...]

    qk = p[:, OFF_Q:OFF_Q + QK_W]
    gms = _group_mean_sq(qk, gones_ref[...])
    qn = (qk * lax.rsqrt(gms + EPS)) * gqk_ref[...]
    cos, sa, sb = cos_ref[...], sa_ref[...], sb_ref[...]
    for j in range(QK_W // LANES):
        xb = qn[:, j * LANES:(j + 1) * LANES]
        up = pltpu.roll(xb, LANES - 16, axis=1)
        dn = pltpu.roll(xb, 16, axis=1)
        qk_ref[0, :, j * LANES:(j + 1) * LANES] = (xb * cos + up * sa + dn * sb).astype(BF16)

    v_ref[0] = p[:, OFF_V:OFF_V + KV_W].T.astype(BF16)
    mqkv_ref[0, :, 0:ML_W] = p[:, OFF_MQ:OFF_MQ + ML_W].astype(BF16)
    mqkv_ref[0, :, ML_W:2 * ML_W] = (p[:, OFF_MQ + ML_W:OFF_MQ + 2 * ML_W] * (HEAD_DIM ** -0.5)).astype(BF16)
    mqkv_ref[0, :, 2 * ML_W:3 * ML_W] = p[:, OFF_MQ + 2 * ML_W:OFF_MQ + 3 * ML_W].astype(BF16)
    mo_ref[0] = p[:, OFF_MO:OFF_MO + ML_W]
    cuv_ref[0] = p[:, OFF_CU:OFF_CU + 2 * CM_W]
    gt_ref[0, 0] = p[:, OFF_GT:OFF_GT + LANES].T[0:N_GATES, :]


def _in_proj(x_all, mod_l, g1, w_in, b_in, gqk, gones, cos_t, sa_t, sb_t, n_ctx_tiles):
    bsz, tt, d = x_all.shape
    tm = ROW_TILE
    nt = tt // tm
    row = lambda b, t: (b, t, 0)
    const = lambda b, t: (0, 0)
    tab = lambda b, t: (t, 0)
    outs = [(QK_W, BF16), (KV_W, BF16), (3 * ML_W, BF16), (ML_W, F32), (2 * CM_W, F32), (LANES, F32)]
    return pl.pallas_call(
        _in_kernel,
        grid=(bsz, nt),
        in_specs=[pl.BlockSpec((1, tm, d), row),
                  pl.BlockSpec((1, 1, 6 * d), lambda b, t: (jnp.where(t < n_ctx_tiles, bsz, b), 0, 0)),
                  pl.BlockSpec((1, d), const),
                  pl.BlockSpec((d, N_PROJ), const),
                  pl.BlockSpec((1, N_PROJ), const),
                  pl.BlockSpec((1, QK_W), const),
                  pl.BlockSpec((LANES, LANES), const),
                  pl.BlockSpec((tm, LANES), tab),
                  pl.BlockSpec((tm, LANES), tab),
                  pl.BlockSpec((tm, LANES), tab)],
        out_specs=[pl.BlockSpec((1, KV_W, tm), lambda b, t: (b, 0, t)) if i == 1
                   else pl.BlockSpec((1, 1, N_GATES, tm), lambda b, t: (b, t, 0, 0)) if i == 5
                   else pl.BlockSpec((1, tm, w), row) for i, (w, _) in enumerate(outs)],
        out_shape=[jax.ShapeDtypeStruct((bsz, KV_W, tt) if i == 1 else (bsz, nt, N_GATES, tm) if i == 5
                                        else (bsz, tt, w), dt) for i, (w, dt) in enumerate(outs)],
        compiler_params=_cparams("parallel", "parallel"),
        name="in_proj",
    )(x_all, mod_l, g1, w_in, b_in, gqk, gones, cos_t, sa_t, sb_t)


def _attn_kernel(q_ref, k_ref, vt_ref, o_ref, s_ref, e_ref, *, n_ctx_tiles, t_ctx):
    t = pl.program_id(1)
    tq = q_ref.shape[1]
    low = lax.broadcasted_iota(jnp.int32, (tq, LANES), 1) < HEAD_DIM
    n_pairs = ATTN_W // LANES
    groups = [(kvh, pp) for pp in range(n_pairs // 2) for kvh in range(2)]

    def attend(kv_len):
        def scores(g):
            k = k_ref[0, 0:kv_len, :]
            kvh, pp = groups[g]
            sel = low if kvh == 0 else jnp.logical_not(low)
            q2 = jnp.concatenate(
                [jnp.where(sel, q_ref[0, :, p * LANES:(p + 1) * LANES], jnp.zeros((tq, LANES), BF16))
                 for p in (2 * pp, 2 * pp + 1)], axis=0)
            s_ref[g % 2, 0:kv_len, :] = lax.dot_general(k, q2, (((1,), (1,)), ((), ())),
                                                        preferred_element_type=F32)

        scores(0)
        outs = {}
        for g, (kvh, pp) in enumerate(groups):
            if g + 1 < len(groups):
                scores(g + 1)
            chunks = [slice(c, c + ATTN_CHUNK) for c in range(0, kv_len, ATTN_CHUNK)]
            fold = lambda a: a.reshape(ATTN_CHUNK // 8, 8, 2 * tq)
            m8 = None
            for rows in chunks:
                mc = jnp.max(fold(s_ref[g % 2, rows, :]), axis=0)
                m8 = mc if m8 is None else jnp.maximum(m8, mc)
            m = jnp.max(m8, axis=0, keepdims=True)
            for rows in chunks:
                e_ref[g % 2, rows, :] = jnp.exp2(s_ref[g % 2, rows, :] - m).astype(BF16)
            vt = vt_ref[0, kvh * HEAD_DIM:(kvh + 1) * HEAD_DIM, 0:kv_len]
            vt1 = jnp.concatenate([vt, jnp.ones((16, kv_len), BF16)], axis=0)
            ov = jnp.dot(vt1, e_ref[g % 2, 0:kv_len, :], preferred_element_type=F32)
            outs[kvh] = ov[0:HEAD_DIM] / ov[HEAD_DIM:HEAD_DIM + 1]
            if kvh == 1:
                for j, p in enumerate((2 * pp, 2 * pp + 1)):
                    both = jnp.concatenate([outs[0][:, j * tq:(j + 1) * tq], outs[1][:, j * tq:(j + 1) * tq]], axis=0)
                    o_ref[0, :, p * LANES:(p + 1) * LANES] = both.T.astype(BF16)

    @pl.when(t < n_ctx_tiles)
    def _():
        attend(t_ctx)

    @pl.when(t >= n_ctx_tiles)
    def _():
        attend(k_ref.shape[1])


def _attention(qk, vt, t_ctx):
    bsz, tt, _ = qk.shape
    tq = ROW_TILE
    kern = functools.partial(_attn_kernel, n_ctx_tiles=t_ctx // tq, t_ctx=t_ctx)
    return pl.pallas_call(
        kern,
        grid=(bsz, tt // tq),
        in_specs=[pl.BlockSpec((1, tq, ATTN_W), lambda b, t: (b, t, 0)),
                  pl.BlockSpec((1, tt, KV_W), lambda b, t: (b, 0, ATTN_W // KV_W)),
                  pl.BlockSpec((1, KV_W, tt), lambda b, t: (b, 0, 0))],
        out_specs=pl.BlockSpec((1, tq, ATTN_W), lambda b, t: (b, t, 0)),
        out_shape=jax.ShapeDtypeStruct((bsz, tt, ATTN_W), BF16),
        scratch_shapes=[pltpu.VMEM((2, tt, 2 * tq), F32), pltpu.VMEM((2, tt, 2 * tq), BF16)],
        compiler_params=_cparams("parallel", "parallel"),
        name="attention",
    )(qk, qk, vt)


def _log_sigmoid(x):
    return jnp.minimum(x, 0.0) - jnp.log1p(jnp.exp(-jnp.abs(x)))


def _split3(a):
    hi = a.astype(BF16)
    r1 = a - hi.astype(F32)
    mid = r1.astype(BF16)
    lo = (r1 - mid.astype(F32)).astype(BF16)
    return hi, mid, lo


def _mlstm_kernel(qkv_ref, gt_ref, mo_ref, gmh_ref, gones_ref, esel_ref, tril_ref, triu_ref,
                  o_ref, hf_ref, hb_ref, cn_ref, m_ref, neg_ref, cmask_ref, *, n_ctx_chunks):
    L = ML_CHUNK
    W = ML_W
    n_chunks = qkv_ref.shape[1] // L

    row = lax.broadcasted_iota(jnp.int32, (L, L), 0)
    col = lax.broadcasted_iota(jnp.int32, (L, L), 1)
    neg_ref[0] = jnp.where(col <= row, 0.0, -jnp.inf)
    neg_ref[1] = jnp.where(col >= row, 0.0, -jnp.inf)
    blk_r = lax.broadcasted_iota(jnp.int32, (W, 2 * W), 0) // HEAD_DIM
    blk_c = (lax.broadcasted_iota(jnp.int32, (W, 2 * W), 1) % W) // HEAD_DIM
    cmask_ref[...] = (blk_r == blk_c).astype(F32)
    low = lax.broadcasted_iota(jnp.int32, (L, LANES), 1) < HEAD_DIM
    ones_blk = jnp.ones((L, W), BF16)
    cn_ref[...] = jnp.zeros_like(cn_ref)
    m_ref[...] = jnp.zeros_like(m_ref)

    def step(rev, c):
        r0 = pl.multiple_of(c * L, L)
        q = qkv_ref[0, pl.ds(r0, L), 0:W]
        k = qkv_ref[0, pl.ds(r0, L), W:2 * W]
        v = qkv_ref[0, pl.ds(r0, L), 2 * W:3 * W]
        gt = gt_ref[0, c]
        goff = 2 * ML_HEADS if rev else 0
        tri_r = tril_ref[...] if rev else triu_ref[...]
        last = 0 if rev else L - 1

        brows = sum(jnp.dot(part, tri_r, preferred_element_type=F32) for part in _split3(_log_sigmoid(gt)))
        bexp = sum(lax.dot_general(part, esel_ref[rev], (((0,), (0,)), ((), ())), preferred_element_type=F32)
                   for part in _split3(brows))
        r_rows = gt[goff:goff + ML_HEADS] - brows[goff + ML_HEADS:goff + 2 * ML_HEADS]

        cn = cn_ref[rev]
        m_prev = m_ref[rev][0:1, :]
        qc = jnp.dot(q, cn.astype(BF16), preferred_element_type=F32)
        kt = k.astype(F32).T
        kt_b = kt.astype(BF16)

        mus, asums, avs, gsr = [], [], [], []
        for h in range(ML_HEADS):
            cols = slice((h // 2) * LANES, (h // 2 + 1) * LANES)
            sel = low if h % 2 == 0 else jnp.logical_not(low)
            qh = jnp.where(sel, q[:, cols], jnp.zeros((L, LANES), BF16))
            s = jnp.dot(qh, kt_b[cols, :], preferred_element_type=F32)
            r = r_rows[h:h + 1, :]
            rm = r + neg_ref[rev]
            mu = jnp.maximum(jnp.max(rm, axis=-1, keepdims=True), m_prev[:, h * HEAD_DIM:h * HEAD_DIM + 1])
            a = jnp.exp(rm - mu) * s
            asums.append(jnp.sum(a, axis=-1, keepdims=True))
            avs.append(jnp.dot(a.astype(BF16), v[:, cols], preferred_element_type=F32))
            mus.append(mu)
            gsr.append(jnp.broadcast_to(jnp.exp(r - mu[last:last + 1, :]), (HEAD_DIM, L)))

        pair = lambda xs: jnp.concatenate([jnp.where(low, xs[0], xs[1]), jnp.where(low, xs[2], xs[3])], axis=1)
        mu_x, asum_x, av_x = pair(mus), pair(asums), pair(avs)
        w_inter = jnp.exp(m_prev - mu_x)
        m_t = bexp + mu_x
        num = w_inter * qc[:, 0:W] + av_x
        den = w_inter * qc[:, W:2 * W] + asum_x
        dst = hb_ref if rev else hf_ref
        dst[pl.ds(r0, L), :] = num / jnp.maximum(jnp.abs(den), jnp.exp(-m_t))

        mu_last = mu_x[last:last + 1, :]
        wc = jnp.exp(m_prev - mu_last)
        gkt = (kt * jnp.concatenate(gsr, axis=0)).astype(BF16)
        upd = jnp.dot(gkt, jnp.concatenate([v, ones_blk], axis=1), preferred_element_type=F32)
        cn_ref[rev] = jnp.concatenate([wc, wc], axis=1) * cn + upd * cmask_ref[...]
        m_ref[rev] = jnp.broadcast_to(bexp[last:last + 1, :] + mu_last, (8, W))

    def body(j, carry):
        step(0, j)
        cb = jnp.where(j < n_ctx_chunks, n_ctx_chunks - 1 - j, n_chunks - 1 - (j - n_ctx_chunks))
        step(1, cb)
        return carry

    lax.fori_loop(0, n_chunks, body, 0)

    def merge(c, carry):
        r0 = pl.multiple_of(c * L, L)
        hs = hf_ref[pl.ds(r0, L), :] + hb_ref[pl.ds(r0, L), :]
        gms = _group_mean_sq(hs, gones_ref[...])
        y = (hs * lax.rsqrt(gms + EPS)) * gmh_ref[...]
        o_ref[0, pl.ds(r0, L), :] = (jax.nn.sigmoid(mo_ref[0, pl.ds(r0, L), :]) * y).astype(BF16)
        return carry

    lax.fori_loop(0, n_chunks, merge, 0)


def _mlstm(mqkv, gates_t, mo, g_mh, gones, esel, tril, triu, t_ctx):
    bsz, tt, _ = mqkv.shape
    L = ML_CHUNK
    nc = tt // L
    kern = functools.partial(_mlstm_kernel, n_ctx_chunks=t_ctx // L)
    per_b = lambda b: (b, 0, 0)
    const = lambda b: (0, 0)
    return pl.pallas_call(
        kern,
        grid=(bsz,),
        in_specs=[pl.BlockSpec((1, tt, 3 * ML_W), per_b),
                  pl.BlockSpec((1, nc, N_GATES, L), lambda b: (b, 0, 0, 0)),
                  pl.BlockSpec((1, tt, ML_W), per_b),
                  pl.BlockSpec((1, ML_W), const),
                  pl.BlockSpec((LANES, LANES), const),
                  pl.BlockSpec((2, N_GATES, ML_W), lambda b: (0, 0, 0)),
                  pl.BlockSpec((L, L), const),
                  pl.BlockSpec((L, L), const)],
        out_specs=pl.BlockSpec((1, tt, ML_W), per_b),
        out_shape=jax.ShapeDtypeStruct((bsz, tt, ML_W), BF16),
        scratch_shapes=[pltpu.VMEM((tt, ML_W), F32), pltpu.VMEM((tt, ML_W), F32),
                        pltpu.VMEM((2, ML_W, 2 * ML_W), F32), pltpu.VMEM((2, 8, ML_W), F32),
                        pltpu.VMEM((2, L, L), F32), pltpu.VMEM((ML_W, 2 * ML_W), F32)],
        compiler_params=_cparams("parallel"),
        name="mlstm",
    )(mqkv, gates_t, mo, g_mh, gones, esel, tril, triu)


def _chunk_mlp(uv, g_v, g_ones, wsp_ref, b_sp):
    lane = lax.broadcasted_iota(jnp.int32, (CM_CHUNK, CM_W), 1)
    u = jax.nn.gelu(uv[:, 0:CM_W])
    v = jax.nn.gelu(uv[:, CM_W:2 * CM_W])
    gms = _group_mean_sq(v, g_ones)
    vb = ((v * lax.rsqrt(gms + EPS)) * g_v).astype(BF16)
    z = jnp.zeros((CM_CHUNK, CM_W), F32)
    for g in range(CM_GROUPS):
        zg = jnp.dot(wsp_ref[g], vb, preferred_element_type=F32)
        z = jnp.where((lane // HEAD_DIM) == g, zg, z)
    return (u * (z + b_sp)).astype(BF16)


def _out_kernel(at_ref, ml_ref, uv_ref, x_ref, mod_ref, g2_ref, w_ref, gv_ref, gones_ref, wsp_ref, bsp_ref,
                xo_ref, h2_ref):
    d = D_MODEL
    tm = x_ref.shape[1]
    mod = mod_ref[0]
    g1, sh2, s2 = mod[:, 2 * d:3 * d], mod[:, 3 * d:4 * d], mod[:, 4 * d:5 * d]
    cm = jnp.concatenate([_chunk_mlp(uv_ref[0, c:c + CM_CHUNK, :], gv_ref[...], gones_ref[...], wsp_ref, bsp_ref[...])
                          for c in range(0, tm, CM_CHUNK)], axis=0)
    mm = (jnp.dot(at_ref[0], w_ref[0:ATTN_W, :], preferred_element_type=F32)
          + jnp.dot(ml_ref[0], w_ref[ATTN_W:ATTN_W + ML_W, :], preferred_element_type=F32)
          + jnp.dot(cm, w_ref[ATTN_W + ML_W:, :], preferred_element_type=F32))
    y = x_ref[0] + g1 * mm
    xo_ref[0] = y
    ms = jnp.mean(y * y, axis=-1, keepdims=True)
    h2_ref[0] = ((y * lax.rsqrt(ms + EPS)) * g2_ref[...] * (1 + s2) + sh2).astype(BF16)


def _out_proj(attn, ml, cuv, x_all, mod_l, g2, w_out, g_v, gones, w_sp, b_sp_lanes, n_ctx_tiles, skip_tiles):
    bsz, tt, d = x_all.shape
    tm = ROW_TILE
    t_out = tt - skip_tiles * tm
    row = lambda b, t: (b, t + skip_tiles, 0)
    out_row = lambda b, t: (b, t, 0)
    const = lambda b, t: (0, 0)
    return pl.pallas_call(
        _out_kernel,
        grid=(bsz, t_out // tm),
        in_specs=[pl.BlockSpec((1, tm, ATTN_W), row),
                  pl.BlockSpec((1, tm, ML_W), row),
                  pl.BlockSpec((1, tm, 2 * CM_W), row),
                  pl.BlockSpec((1, tm, d), row),
                  pl.BlockSpec((1, 1, 6 * d), lambda b, t: (jnp.where(t + skip_tiles < n_ctx_tiles, bsz, b), 0, 0)),
                  pl.BlockSpec((1, d), const),
                  pl.BlockSpec((d, d), const),
                  pl.BlockSpec((1, CM_W), const),
                  pl.BlockSpec((LANES, LANES), const),
                  pl.BlockSpec((CM_GROUPS, CM_CHUNK, CM_CHUNK), lambda b, t: (0, 0, 0)),
                  pl.BlockSpec((CM_CHUNK, CM_W), const)],
        out_specs=[pl.BlockSpec((1, tm, d), out_row), pl.BlockSpec((1, tm, d), out_row)],
        out_shape=[jax.ShapeDtypeStruct((bsz, t_out, d), F32), jax.ShapeDtypeStruct((bsz, t_out, d), BF16)],
        compiler_params=_cparams("parallel", "parallel"),
        name="out_proj",
    )(attn, ml, cuv, x_all, mod_l, g2, w_out, g_v, gones, w_sp, b_sp_lanes)


def _ffn_kernel(h_ref, hp_ref, hn_ref, x_ref, mod_ref, wu_ref, cw_ref, wd_ref, o_ref, hcat_ref, act_ref, *,
                n_ctx_tiles):
    d = D_MODEL
    tm = h_ref.shape[1]
    t = pl.program_id(1)
    nt = pl.num_programs(1)
    has_prev = jnp.logical_and(t != 0, t != n_ctx_tiles)
    has_next = jnp.logical_and(t != n_ctx_tiles - 1, t != nt - 1)
    hrow = lax.broadcasted_iota(jnp.int32, (HALO, d), 0)
    before = pltpu.roll(hp_ref[0].astype(F32), 1, axis=0)
    after = pltpu.roll(hn_ref[0].astype(F32), 7, axis=0)
    halo = jnp.where(jnp.logical_and(hrow == 0, has_prev), before,
                     jnp.where(jnp.logical_and(hrow == 7, has_next), after, 0.0))
    hcat_ref[0:tm, :] = h_ref[0]
    hcat_ref[tm:tm + HALO, :] = halo.astype(BF16)
    hc = hcat_ref[...]
    sub = lax.broadcasted_iota(jnp.int32, (8, FF_TILE), 0)

    for f in range(D_FF // FF_TILE):
        halves = []
        for off in (f * FF_TILE, D_FF + f * FF_TILE):
            sl = slice(off, off + FF_TILE)
            a = jnp.dot(hc, wu_ref[:, sl], preferred_element_type=F32)
            cur = a[0:tm]
            edge = a[tm:tm + 8]
            dn = pltpu.roll(cur, 1, axis=0)
            up = pltpu.roll(cur, tm - 1, axis=0)
            prev = jnp.concatenate([jnp.where(sub == 0, edge, dn[0:8]), dn[8:]], axis=0)
            nxt = jnp.concatenate([up[:tm - 8], jnp.where(sub == 7, edge, up[tm - 8:])], axis=0)
            halves.append(prev * cw_ref[0:1, sl] + cur * cw_ref[1:2, sl] + nxt * cw_ref[2:3, sl] + cw_ref[3:4, sl])
        gate, val = halves
        act_ref[:, f * FF_TILE:(f + 1) * FF_TILE] = ((gate * jax.nn.sigmoid(gate)) * val).astype(BF16)
    acc = jnp.dot(act_ref[...], wd_ref[...], preferred_element_type=F32)
    g2 = mod_ref[0][:, 5 * d:6 * d]
    o_ref[0] = x_ref[0] + g2 * acc


def _conv_ffn(h2, x_mid, mod_l, w_up, cw, w_down, n_ctx_tiles):
    bsz, tt, d = x_mid.shape
    tm = ROW_TILE
    nt = tt // tm
    hb = tm // HALO
    n_halo_blocks = tt // HALO
    row = lambda b, t: (b, t, 0)
    const = lambda b, t: (0, 0)
    kern = functools.partial(_ffn_kernel, n_ctx_tiles=n_ctx_tiles)
    return pl.pallas_call(
        kern,
        grid=(bsz, nt),
        in_specs=[pl.BlockSpec((1, tm, d), row),
                  pl.BlockSpec((1, HALO, d), lambda b, t: (b, jnp.maximum(t * hb - 1, 0), 0)),
                  pl.BlockSpec((1, HALO, d), lambda b, t: (b, jnp.minimum((t + 1) * hb, n_halo_blocks - 1), 0)),
                  pl.BlockSpec((1, tm, d), row),
                  pl.BlockSpec((1, 1, 6 * d), lambda b, t: (jnp.where(t < n_ctx_tiles, bsz, b), 0, 0)),
                  pl.BlockSpec((d, 2 * D_FF), const),
                  pl.BlockSpec((8, 2 * D_FF), const),
                  pl.BlockSpec((D_FF, d), const)],
        out_specs=pl.BlockSpec((1, tm, d), row),
        out_shape=jax.ShapeDtypeStruct((bsz, tt, d), F32),
        scratch_shapes=[pltpu.VMEM((tm + HALO, d), BF16), pltpu.VMEM((tm, D_FF), BF16)],
        compiler_params=_cparams("parallel", "parallel"),
        name="conv_ffn",
    )(h2, h2, h2, x_mid, mod_l, w_up, cw, w_down)


def _rope_tables(s_len, t_ctx):
    nf = HEAD_DIM // 4
    rows = s_len // GRID_W
    pos = jnp.arange(s_len)
    row = jnp.repeat(jnp.arange(rows), GRID_W).astype(F32)
    colp = (pos % GRID_W).astype(F32)
    inv = ROPE_THETA ** (-jnp.arange(nf, dtype=F32) / nf)
    ang_r, ang_c = row[:, None] * inv[None], colp[:, None] * inv[None]
    cr, cc, sr, sc = jnp.cos(ang_r), jnp.cos(ang_c), jnp.sin(ang_r), jnp.sin(ang_c)
    z = jnp.zeros_like(sr)
    cos = jnp.concatenate([cr, cr, cc, cc], axis=1)
    sa = jnp.concatenate([-sr, z, -sc, z], axis=1)
    sb = jnp.concatenate([z, sr, z, sc], axis=1)
    pad = lambda a, fill: jnp.concatenate([jnp.full((t_ctx, HEAD_DIM), fill, F32), a], axis=0)
    two = lambda a: jnp.concatenate([a, a], axis=1)
    return two(pad(cos, 1.0)), two(pad(sa, 0.0)), two(pad(sb, 0.0))


def kernel(x, c, ctx, c_ctx, w_ada, b_ada, g_norm1, w_in, b_in, g_q, g_k, g_mh, g_v, w_sp, b_sp,
           w_out, g_norm2, w_up, conv_w, conv_b, w_down):
    bsz, s_len, d = x.shape
    t_ctx = ctx.shape[1]
    depth = w_ada.shape[0]
    tt = t_ctx + s_len
    assert d == D_MODEL and t_ctx % ML_CHUNK == 0 and s_len % ML_CHUNK == 0 and ROW_TILE == ML_CHUNK
    n_ctx_tiles = t_ctx // ROW_TILE

    o = ATTN_W + 2 * KV_W
    gate_off = o + 4 * ML_W
    cm_off = gate_off + N_GATES
    segs = ([(h * HEAD_DIM, (h + 1) * HEAD_DIM) for h in Q_HEAD_ORDER]
            + [(ATTN_W, gate_off), (cm_off, cm_off + 2 * CM_W), (gate_off, gate_off + N_GATES)])
    pad = N_PROJ - sum(e - s for s, e in segs)
    w_in_p = jnp.concatenate([w_in[:, :, s:e].astype(BF16) for s, e in segs]
                             + [jnp.zeros((depth, d, pad), BF16)], axis=2)
    b_in_p = jnp.concatenate([b_in[:, s:e] for s, e in segs] + [jnp.zeros((depth, pad), F32)], axis=1)[:, None, :]
    gqk = jnp.concatenate([jnp.tile(g_q, (1, N_Q_HEADS)) * (HEAD_DIM ** -0.5 * LOG2_E), jnp.tile(g_k, (1, N_KV_HEADS))],
                          axis=1)[:, None, :]
    w_out_p = jnp.concatenate([w_out[:, h * HEAD_DIM:(h + 1) * HEAD_DIM, :].astype(BF16) for h in Q_HEAD_ORDER]
                              + [w_out[:, ATTN_W:, :].astype(BF16)], axis=1)
    w_up_b = w_up.astype(BF16)
    w_down_b = w_down.astype(BF16)
    cw = jnp.concatenate([conv_w, conv_b[:, None, :], jnp.zeros((depth, 4, 2 * D_FF), F32)], axis=1)
    w_sp_b = w_sp.astype(BF16)
    b_sp_l = jnp.repeat(jnp.swapaxes(b_sp, 1, 2), HEAD_DIM, axis=2)
    blk = np.arange(LANES) // HEAD_DIM
    gones = jnp.asarray(blk[:, None] == blk[None, :], BF16)
    ii = np.arange(ML_CHUNK)
    tril = jnp.asarray(ii[None, :] <= ii[:, None], BF16)
    triu = jnp.asarray(ii[None, :] >= ii[:, None], BF16)
    gate_row = np.arange(N_GATES)[:, None]
    lane_head = (np.arange(ML_W) // HEAD_DIM)[None, :]
    esel = jnp.asarray(np.stack([gate_row == ML_HEADS + lane_head, gate_row == 3 * ML_HEADS + lane_head]), BF16)
    cos_t, sa_t, sb_t = _rope_tables(s_len, t_ctx)

    n_rows = -(-(bsz + 1) // 8) * 8
    cs = jnp.concatenate([c, c_ctx[None, :], jnp.zeros((n_rows - bsz - 1, d), F32)], axis=0)
    mod = _modulation(cs, w_ada, b_ada)

    x_all = jnp.concatenate([ctx, x], axis=1)
    for l in range(depth):
        mod_l = mod[l][:, None, :]
        qk, v, mqkv, mo, cuv, gates_t = _in_proj(x_all, mod_l, g_norm1[l][None], w_in_p[l], b_in_p[l], gqk[l], gones,
                                             cos_t, sa_t, sb_t, n_ctx_tiles)
        attn = _attention(qk, v, t_ctx)
        ml = _mlstm(mqkv, gates_t, mo, g_mh[l][None], gones, esel, tril, triu, t_ctx)
        skip = n_ctx_tiles if l == depth - 1 else 0
        x_mid, h2 = _out_proj(attn, ml, cuv, x_all, mod_l, g_norm2[l][None], w_out_p[l], g_v[l][None], gones,
                              w_sp_b[l], b_sp_l[l], n_ctx_tiles, skip)
        x_all = _conv_ffn(h2, x_mid, mod_l, w_up_b[l], cw[l], w_down_b[l], n_ctx_tiles - skip)
    return x_all
```

```python
import functools

import jax
import jax.numpy as jnp
import numpy as np
from jax import lax
from jax.experimental import pallas as pl
from jax.experimental.pallas import tpu as pltpu

F32 = jnp.float32
BF16 = jnp.bfloat16

D_MODEL = 1024
HEAD_DIM = 64
N_Q_HEADS = 8
N_KV_HEADS = 2
GRID_W = 64
ROPE_THETA = 10000.0
EPS = 1e-6
ATTN_W = N_Q_HEADS * HEAD_DIM
KV_W = N_KV_HEADS * HEAD_DIM
ML_HEADS = 4
ML_W = ML_HEADS * HEAD_DIM
CM_GROUPS = 4
CM_W = CM_GROUPS * HEAD_DIM
CM_CHUNK = 128
N_GATES = 4 * ML_HEADS
D_FF = 2816
LANES = 128

OFF_Q, OFF_K, OFF_V = 0, 512, 640
OFF_MQ, OFF_MO, OFF_CU, OFF_GT = 768, 1536, 1792, 2304
N_PROJ = 2432
QK_W = ATTN_W + KV_W

ROW_TILE = 256
ML_CHUNK = 256
FF_TILE = 256
HALO = 16
ATTN_CHUNK = 128
LOG2_E = 1.4426950408889634
VMEM_LIMIT = 56 * 1024 * 1024

Q_HEAD_ORDER = (0, 4, 1, 5, 2, 6, 3, 7)


def _cparams(*sem):
    return pltpu.CompilerParams(dimension_semantics=sem, vmem_limit_bytes=VMEM_LIMIT)


def _group_mean_sq(x, g_ones):
    x2 = x * x
    hi = x2.astype(BF16)
    lo = (x2 - hi.astype(F32)).astype(BF16)
    g2 = jnp.concatenate([g_ones, g_ones], axis=0)
    cols = []
    for j in range(x.shape[1] // LANES):
        sl = slice(j * LANES, (j + 1) * LANES)
        cols.append(jnp.dot(jnp.concatenate([hi[:, sl], lo[:, sl]], axis=1), g2, preferred_element_type=F32))
    s = cols[0] if len(cols) == 1 else jnp.concatenate(cols, axis=1)
    return s * (1.0 / HEAD_DIM)


def _mod_kernel(c_ref, w_ref, b_ref, o_ref):
    c = c_ref[...]
    sc = c * jax.nn.sigmoid(c)
    o_ref[0] = jnp.dot(sc, w_ref[0], preferred_element_type=F32,
                       precision=lax.Precision.HIGHEST) + b_ref[0]


def _modulation(cs, w_ada, b_ada):
    depth, d, d6 = w_ada.shape
    r = cs.shape[0]
    nj = d6 // d
    return pl.pallas_call(
        _mod_kernel,
        grid=(depth, nj),
        in_specs=[pl.BlockSpec((r, d), lambda l, j: (0, 0)),
                  pl.BlockSpec((1, d, d), lambda l, j: (l, 0, j)),
                  pl.BlockSpec((1, 1, d), lambda l, j: (l, 0, j))],
        out_specs=pl.BlockSpec((1, r, d), lambda l, j: (l, 0, j)),
        out_shape=jax.ShapeDtypeStruct((depth, r, d6), F32),
        compiler_params=_cparams("arbitrary", "arbitrary"),
        name="adaln_mod",
    )(cs, w_ada, b_ada.reshape(depth, 1, d6))


def _stream_specs(stream, n_ctx_tiles, skip_tiles=0):
    d = D_MODEL
    if not isinstance(stream, tuple):
        return [stream], [pl.BlockSpec((1, ROW_TILE, d), lambda b, t: (b, t + skip_tiles, 0))]
    return list(stream), [
        pl.BlockSpec((1, ROW_TILE, d), lambda b, t: (b, jnp.minimum(t + skip_tiles, n_ctx_tiles - 1), 0)),
        pl.BlockSpec((1, ROW_TILE, d), lambda b, t: (b, jnp.maximum(t + skip_tiles - n_ctx_tiles, 0), 0))]


def _stream_tile(refs, n_ctx_tiles, skip_tiles=0):
    if len(refs) == 1:
        return refs[0][0]
    return jnp.where(pl.program_id(1) + skip_tiles < n_ctx_tiles, refs[0][0], refs[1][0])


def _in_kernel(*refs, n_stream, n_ctx_tiles):
    (mod_ref, g1_ref, w_ref, b_ref, gqk_ref, gones_ref, cos_ref, sa_ref, sb_ref,
     qk_ref, v_ref, mqkv_ref, mo_ref, cuv_ref, gt_ref) = refs[n_stream:]
    d = D_MODEL
    x = _stream_tile(refs[:n_stream], n_ctx_tiles)
    mod = mod_ref[0]
    sh1, s1 = mod[:, 0:d], mod[:, d:2 * d]
    ms = jnp.mean(x * x, axis=-1, keepdims=True)
    h = (x * lax.rsqrt(ms + EPS)) * g1_ref[...] * (1 + s1) + sh1
    hb = h.astype(BF16)
    p = jnp.concatenate([jnp.dot(hb, w_ref[:, 0:OFF_MQ], preferred_element_type=F32),
                         jnp.dot(hb, w_ref[:, OFF_MQ:], preferred_element_type=F32)], axis=1) + b_ref[...]

    qk = p[:, OFF_Q:OFF_Q + QK_W]
    gms = _group_mean_sq(qk, gones_ref[...])
    qn = (qk * lax.rsqrt(gms + EPS)) * gqk_ref[...]
    cos, sa, sb = cos_ref[...], sa_ref[...], sb_ref[...]
    for j in range(QK_W // LANES):
        xb = qn[:, j * LANES:(j + 1) * LANES]
        up = pltpu.roll(xb, LANES - 16, axis=1)
        dn = pltpu.roll(xb, 16, axis=1)
        qk_ref[0, :, j * LANES:(j + 1) * LANES] = (xb * cos + up * sa + dn * sb).astype(BF16)

    v_ref[0] = p[:, OFF_V:OFF_V + KV_W].T.astype(BF16)
    mqkv_ref[0, :, 0:ML_W] = p[:, OFF_MQ:OFF_MQ + ML_W].astype(BF16)
    mqkv_ref[0, :, ML_W:2 * ML_W] = (p[:, OFF_MQ + ML_W:OFF_MQ + 2 * ML_W] * (HEAD_DIM ** -0.5)).astype(BF16)
    mqkv_ref[0, :, 2 * ML_W:3 * ML_W] = p[:, OFF_MQ + 2 * ML_W:OFF_MQ + 3 * ML_W].astype(BF16)
    mo_ref[0] = p[:, OFF_MO:OFF_MO + ML_W]
    cuv_ref[0] = p[:, OFF_CU:OFF_CU + 2 * CM_W]
    gt_ref[0, 0] = p[:, OFF_GT:OFF_GT + LANES].T[0:N_GATES, :]


def _in_proj(stream, tt, mod_l, g1, w_in, b_in, gqk, gones, cos_t, sa_t, sb_t, n_ctx_tiles):
    x_arrays, x_specs = _stream_specs(stream, n_ctx_tiles)
    bsz, d = x_arrays[0].shape[0], D_MODEL
    tm = ROW_TILE
    nt = tt // tm
    row = lambda b, t: (b, t, 0)
    const = lambda b, t: (0, 0)
    tab = lambda b, t: (t, 0)
    outs = [(QK_W, BF16), (KV_W, BF16), (3 * ML_W, BF16), (ML_W, F32), (2 * CM_W, F32), (LANES, F32)]
    return pl.pallas_call(
        functools.partial(_in_kernel, n_stream=len(x_arrays), n_ctx_tiles=n_ctx_tiles),
        grid=(bsz, nt),
        in_specs=x_specs + [
                  pl.BlockSpec((1, 1, 6 * d), lambda b, t: (jnp.where(t < n_ctx_tiles, bsz, b), 0, 0)),
                  pl.BlockSpec((1, d), const),
                  pl.BlockSpec((d, N_PROJ), const),
                  pl.BlockSpec((1, N_PROJ), const),
                  pl.BlockSpec((1, QK_W), const),
                  pl.BlockSpec((LANES, LANES), const),
                  pl.BlockSpec((tm, LANES), tab),
                  pl.BlockSpec((tm, LANES), tab),
                  pl.BlockSpec((tm, LANES), tab)],
        out_specs=[pl.BlockSpec((1, KV_W, tm), lambda b, t: (b, 0, t)) if i == 1
                   else pl.BlockSpec((1, 1, N_GATES, tm), lambda b, t: (b, t, 0, 0)) if i == 5
                   else pl.BlockSpec((1, tm, w), row) for i, (w, _) in enumerate(outs)],
        out_shape=[jax.ShapeDtypeStruct((bsz, KV_W, tt) if i == 1 else (bsz, nt, N_GATES, tm) if i == 5
                                        else (bsz, tt, w), dt) for i, (w, dt) in enumerate(outs)],
        compiler_params=_cparams("parallel", "parallel"),
        name="in_proj",
    )(*x_arrays, mod_l, g1, w_in, b_in, gqk, gones, cos_t, sa_t, sb_t)


def _attn_kernel(q_ref, k_ref, vt_ref, o_ref, s_ref, e_ref, *, n_ctx_tiles, t_ctx):
    t = pl.program_id(1)
    tq = q_ref.shape[1]
    low = lax.broadcasted_iota(jnp.int32, (tq, LANES), 1) < HEAD_DIM
    n_pairs = ATTN_W // LANES
    groups = [(kvh, pp) for pp in range(n_pairs // 2) for kvh in range(2)]

    def attend(kv_len):
        def scores(g):
            k = k_ref[0, 0:kv_len, :]
            kvh, pp = groups[g]
            sel = low if kvh == 0 else jnp.logical_not(low)
            q2 = jnp.concatenate(
                [jnp.where(sel, q_ref[0, :, p * LANES:(p + 1) * LANES], jnp.zeros((tq, LANES), BF16))
                 for p in (2 * pp, 2 * pp + 1)], axis=0)
            s_ref[g % 2, 0:kv_len, :] = lax.dot_general(k, q2, (((1,), (1,)), ((), ())),
                                                        preferred_element_type=F32)

        scores(0)
        outs = {}
        for g, (kvh, pp) in enumerate(groups):
            if g + 1 < len(groups):
                scores(g + 1)
            chunks = [slice(c, c + ATTN_CHUNK) for c in range(0, kv_len, ATTN_CHUNK)]
            fold = lambda a: a.reshape(ATTN_CHUNK // 8, 8, 2 * tq)
            m8 = None
            for rows in chunks:
                mc = jnp.max(fold(s_ref[g % 2, rows, :]), axis=0)
                m8 = mc if m8 is None else jnp.maximum(m8, mc)
            m = jnp.max(m8, axis=0, keepdims=True)
            for rows in chunks:
                e_ref[g % 2, rows, :] = jnp.exp2(s_ref[g % 2, rows, :] - m).astype(BF16)
            vt = vt_ref[0, kvh * HEAD_DIM:(kvh + 1) * HEAD_DIM, 0:kv_len]
            vt1 = jnp.concatenate([vt, jnp.ones((16, kv_len), BF16)], axis=0)
            ov = jnp.dot(vt1, e_ref[g % 2, 0:kv_len, :], preferred_element_type=F32)
            outs[kvh] = ov[0:HEAD_DIM] / ov[HEAD_DIM:HEAD_DIM + 1]
            if kvh == 1:
                for j, p in enumerate((2 * pp, 2 * pp + 1)):
                    both = jnp.concatenate([outs[0][:, j * tq:(j + 1) * tq], outs[1][:, j * tq:(j + 1) * tq]], axis=0)
                    o_ref[0, :, p * LANES:(p + 1) * LANES] = both.T.astype(BF16)

    @pl.when(t < n_ctx_tiles)
    def _():
        attend(t_ctx)

    @pl.when(t >= n_ctx_tiles)
    def _():
        attend(k_ref.shape[1])


def _attention(qk, vt, t_ctx):
    bsz, tt, _ = qk.shape
    tq = ROW_TILE
    kern = functools.partial(_attn_kernel, n_ctx_tiles=t_ctx // tq, t_ctx=t_ctx)
    return pl.pallas_call(
        kern,
        grid=(bsz, tt // tq),
        in_specs=[pl.BlockSpec((1, tq, ATTN_W), lambda b, t: (b, t, 0)),
                  pl.BlockSpec((1, tt, KV_W), lambda b, t: (b, 0, ATTN_W // KV_W)),
                  pl.BlockSpec((1, KV_W, tt), lambda b, t: (b, 0, 0))],
        out_specs=pl.BlockSpec((1, tq, ATTN_W), lambda b, t: (b, t, 0)),
        out_shape=jax.ShapeDtypeStruct((bsz, tt, ATTN_W), BF16),
        scratch_shapes=[pltpu.VMEM((2, tt, 2 * tq), F32), pltpu.VMEM((2, tt, 2 * tq), BF16)],
        compiler_params=_cparams("parallel", "parallel"),
        name="attention",
    )(qk, qk, vt)


def _log_sigmoid(x):
    return jnp.minimum(x, 0.0) - jnp.log1p(jnp.exp(-jnp.abs(x)))


def _split3(a):
    hi = a.astype(BF16)
    r1 = a - hi.astype(F32)
    mid = r1.astype(BF16)
    lo = (r1 - mid.astype(F32)).astype(BF16)
    return hi, mid, lo


def _mlstm_kernel(qkv_ref, gt_ref, mo_ref, gmh_ref, gones_ref, esel_ref, tril_ref, triu_ref,
                  o_ref, hf_ref, hb_ref, cn_ref, m_ref, neg_ref, cmask_ref, *, n_ctx_chunks):
    L = ML_CHUNK
    W = ML_W
    n_chunks = qkv_ref.shape[1] // L

    row = lax.broadcasted_iota(jnp.int32, (L, L), 0)
    col = lax.broadcasted_iota(jnp.int32, (L, L), 1)
    neg_ref[0] = jnp.where(col <= row, 0.0, -jnp.inf)
    neg_ref[1] = jnp.where(col >= row, 0.0, -jnp.inf)
    blk_r = lax.broadcasted_iota(jnp.int32, (W, 2 * W), 0) // HEAD_DIM
    blk_c = (lax.broadcasted_iota(jnp.int32, (W, 2 * W), 1) % W) // HEAD_DIM
    cmask_ref[...] = (blk_r == blk_c).astype(F32)
    low = lax.broadcasted_iota(jnp.int32, (L, LANES), 1) < HEAD_DIM
    ones_blk = jnp.ones((L, W), BF16)
    cn_ref[...] = jnp.zeros_like(cn_ref)
    m_ref[...] = jnp.zeros_like(m_ref)

    def step(rev, c):
        r0 = pl.multiple_of(c * L, L)
        q = qkv_ref[0, pl.ds(r0, L), 0:W]
        k = qkv_ref[0, pl.ds(r0, L), W:2 * W]
        v = qkv_ref[0, pl.ds(r0, L), 2 * W:3 * W]
        gt = gt_ref[0, c]
        goff = 2 * ML_HEADS if rev else 0
        tri_r = tril_ref[...] if rev else triu_ref[...]
        last = 0 if rev else L - 1

        brows = sum(jnp.dot(part, tri_r, preferred_element_type=F32) for part in _split3(_log_sigmoid(gt)))
        bexp = sum(lax.dot_general(part, esel_ref[rev], (((0,), (0,)), ((), ())), preferred_element_type=F32)
                   for part in _split3(brows))
        r_rows = gt[goff:goff + ML_HEADS] - brows[goff + ML_HEADS:goff + 2 * ML_HEADS]

        cn = cn_ref[rev]
        m_prev = m_ref[rev][0:1, :]
        qc = jnp.dot(q, cn.astype(BF16), preferred_element_type=F32)
        kt = k.astype(F32).T
        kt_b = kt.astype(BF16)

        mus, asums, avs, gsr = [], [], [], []
        for h in range(ML_HEADS):
            cols = slice((h // 2) * LANES, (h // 2 + 1) * LANES)
            sel = low if h % 2 == 0 else jnp.logical_not(low)
            qh = jnp.where(sel, q[:, cols], jnp.zeros((L, LANES), BF16))
            s = jnp.dot(qh, kt_b[cols, :], preferred_element_type=F32)
            r = r_rows[h:h + 1, :]
            rm = r + neg_ref[rev]
            mu = jnp.maximum(jnp.max(rm, axis=-1, keepdims=True), m_prev[:, h * HEAD_DIM:h * HEAD_DIM + 1])
            a = jnp.exp(rm - mu) * s
            asums.append(jnp.sum(a, axis=-1, keepdims=True))
            avs.append(jnp.dot(a.astype(BF16), v[:, cols], preferred_element_type=F32))
            mus.append(mu)
            gsr.append(jnp.broadcast_to(jnp.exp(r - mu[last:last + 1, :]), (HEAD_DIM, L)))

        pair = lambda xs: jnp.concatenate([jnp.where(low, xs[0], xs[1]), jnp.where(low, xs[2], xs[3])], axis=1)
        mu_x, asum_x, av_x = pair(mus), pair(asums), pair(avs)
        w_inter = jnp.exp(m_prev - mu_x)
        m_t = bexp + mu_x
        num = w_inter * qc[:, 0:W] + av_x
        den = w_inter * qc[:, W:2 * W] + asum_x
        dst = hb_ref if rev else hf_ref
        dst[pl.ds(r0, L), :] = num / jnp.maximum(jnp.abs(den), jnp.exp(-m_t))

        mu_last = mu_x[last:last + 1, :]
        wc = jnp.exp(m_prev - mu_last)
        gkt = (kt * jnp.concatenate(gsr, axis=0)).astype(BF16)
        upd = jnp.dot(gkt, jnp.concatenate([v, ones_blk], axis=1), preferred_element_type=F32)
        cn_ref[rev] = jnp.concatenate([wc, wc], axis=1) * cn + upd * cmask_ref[...]
        m_ref[rev] = jnp.broadcast_to(bexp[last:last + 1, :] + mu_last, (8, W))

    def body(j, carry):
        step(0, j)
        cb = jnp.where(j < n_ctx_chunks, n_ctx_chunks - 1 - j, n_chunks - 1 - (j - n_ctx_chunks))
        step(1, cb)
        return carry

    lax.fori_loop(0, n_chunks, body, 0)

    def merge(c, carry):
        r0 = pl.multiple_of(c * L, L)
        hs = hf_ref[pl.ds(r0, L), :] + hb_ref[pl.ds(r0, L), :]
        gms = _group_mean_sq(hs, gones_ref[...])
        y = (hs * lax.rsqrt(gms + EPS)) * gmh_ref[...]
        o_ref[0, pl.ds(r0, L), :] = (jax.nn.sigmoid(mo_ref[0, pl.ds(r0, L), :]) * y).astype(BF16)
        return carry

    lax.fori_loop(0, n_chunks, merge, 0)


def _mlstm(mqkv, gates_t, mo, g_mh, gones, esel, tril, triu, t_ctx):
    bsz, tt, _ = mqkv.shape
    L = ML_CHUNK
    nc = tt // L
    kern = functools.partial(_mlstm_kernel, n_ctx_chunks=t_ctx // L)
    per_b = lambda b: (b, 0, 0)
    const = lambda b: (0, 0)
    return pl.pallas_call(
        kern,
        grid=(bsz,),
        in_specs=[pl.BlockSpec((1, tt, 3 * ML_W), per_b),
                  pl.BlockSpec((1, nc, N_GATES, L), lambda b: (b, 0, 0, 0)),
                  pl.BlockSpec((1, tt, ML_W), per_b),
                  pl.BlockSpec((1, ML_W), const),
                  pl.BlockSpec((LANES, LANES), const),
                  pl.BlockSpec((2, N_GATES, ML_W), lambda b: (0, 0, 0)),
                  pl.BlockSpec((L, L), const),
                  pl.BlockSpec((L, L), const)],
        out_specs=pl.BlockSpec((1, tt, ML_W), per_b),
        out_shape=jax.ShapeDtypeStruct((bsz, tt, ML_W), BF16),
        scratch_shapes=[pltpu.VMEM((tt, ML_W), F32), pltpu.VMEM((tt, ML_W), F32),
                        pltpu.VMEM((2, ML_W, 2 * ML_W), F32), pltpu.VMEM((2, 8, ML_W), F32),
                        pltpu.VMEM((2, L, L), F32), pltpu.VMEM((ML_W, 2 * ML_W), F32)],
        compiler_params=_cparams("parallel"),
        name="mlstm",
    )(mqkv, gates_t, mo, g_mh, gones, esel, tril, triu)


def _chunk_mlp(uv, g_v, g_ones, wsp_ref, b_sp):
    lane = lax.broadcasted_iota(jnp.int32, (CM_CHUNK, CM_W), 1)
    u = jax.nn.gelu(uv[:, 0:CM_W])
    v = jax.nn.gelu(uv[:, CM_W:2 * CM_W])
    gms = _group_mean_sq(v, g_ones)
    vb = ((v * lax.rsqrt(gms + EPS)) * g_v).astype(BF16)
    z = jnp.zeros((CM_CHUNK, CM_W), F32)
    for g in range(CM_GROUPS):
        zg = jnp.dot(wsp_ref[g], vb, preferred_element_type=F32)
        z = jnp.where((lane // HEAD_DIM) == g, zg, z)
    return (u * (z + b_sp)).astype(BF16)


def _out_kernel(*refs, n_stream, n_ctx_tiles, skip_tiles):
    (at_ref, ml_ref, uv_ref, mod_ref, g2_ref, w_ref, gv_ref, gones_ref, wsp_ref, bsp_ref,
     xo_ref, h2_ref) = refs[n_stream:]
    d = D_MODEL
    tm = at_ref.shape[1]
    mod = mod_ref[0]
    g1, sh2, s2 = mod[:, 2 * d:3 * d], mod[:, 3 * d:4 * d], mod[:, 4 * d:5 * d]
    cm = jnp.concatenate([_chunk_mlp(uv_ref[0, c:c + CM_CHUNK, :], gv_ref[...], gones_ref[...], wsp_ref, bsp_ref[...])
                          for c in range(0, tm, CM_CHUNK)], axis=0)
    mm = (jnp.dot(at_ref[0], w_ref[0:ATTN_W, :], preferred_element_type=F32)
          + jnp.dot(ml_ref[0], w_ref[ATTN_W:ATTN_W + ML_W, :], preferred_element_type=F32)
          + jnp.dot(cm, w_ref[ATTN_W + ML_W:, :], preferred_element_type=F32))
    y = _stream_tile(refs[:n_stream], n_ctx_tiles, skip_tiles) + g1 * mm
    xo_ref[0] = y
    ms = jnp.mean(y * y, axis=-1, keepdims=True)
    h2_ref[0] = ((y * lax.rsqrt(ms + EPS)) * g2_ref[...] * (1 + s2) + sh2).astype(BF16)


def _out_proj(attn, ml, cuv, stream, mod_l, g2, w_out, g_v, gones, w_sp, b_sp_lanes, n_ctx_tiles, skip_tiles):
    x_arrays, x_specs = _stream_specs(stream, n_ctx_tiles, skip_tiles)
    bsz, tt, d = attn.shape[0], attn.shape[1], D_MODEL
    tm = ROW_TILE
    t_out = tt - skip_tiles * tm
    row = lambda b, t: (b, t + skip_tiles, 0)
    out_row = lambda b, t: (b, t, 0)
    const = lambda b, t: (0, 0)
    return pl.pallas_call(
        functools.partial(_out_kernel, n_stream=len(x_arrays), n_ctx_tiles=n_ctx_tiles, skip_tiles=skip_tiles),
        grid=(bsz, t_out // tm),
        in_specs=x_specs + [
                  pl.BlockSpec((1, tm, ATTN_W), row),
                  pl.BlockSpec((1, tm, ML_W), row),
                  pl.BlockSpec((1, tm, 2 * CM_W), row),
                  pl.BlockSpec((1, 1, 6 * d), lambda b, t: (jnp.where(t + skip_tiles < n_ctx_tiles, bsz, b), 0, 0)),
                  pl.BlockSpec((1, d), const),
                  pl.BlockSpec((d, d), const),
                  pl.BlockSpec((1, CM_W), const),
                  pl.BlockSpec((LANES, LANES), const),
                  pl.BlockSpec((CM_GROUPS, CM_CHUNK, CM_CHUNK), lambda b, t: (0, 0, 0)),
                  pl.BlockSpec((CM_CHUNK, CM_W), const)],
        out_specs=[pl.BlockSpec((1, tm, d), out_row), pl.BlockSpec((1, tm, d), out_row)],
        out_shape=[jax.ShapeDtypeStruct((bsz, t_out, d), F32), jax.ShapeDtypeStruct((bsz, t_out, d), BF16)],
        compiler_params=_cparams("parallel", "parallel"),
        name="out_proj",
    )(*x_arrays, attn, ml, cuv, mod_l, g2, w_out, g_v, gones, w_sp, b_sp_lanes)


def _ffn_kernel(h_ref, hp_ref, hn_ref, x_ref, mod_ref, wu_ref, cw_ref, wd_ref, o_ref, hcat_ref, act_ref, *,
                n_ctx_tiles):
    d = D_MODEL
    tm = h_ref.shape[1]
    t = pl.program_id(1)
    nt = pl.num_programs(1)
    has_prev = jnp.logical_and(t != 0, t != n_ctx_tiles)
    has_next = jnp.logical_and(t != n_ctx_tiles - 1, t != nt - 1)
    hrow = lax.broadcasted_iota(jnp.int32, (HALO, d), 0)
    before = pltpu.roll(hp_ref[0].astype(F32), 1, axis=0)
    after = pltpu.roll(hn_ref[0].astype(F32), 7, axis=0)
    halo = jnp.where(jnp.logical_and(hrow == 0, has_prev), before,
                     jnp.where(jnp.logical_and(hrow == 7, has_next), after, 0.0))
    hcat_ref[0:tm, :] = h_ref[0]
    hcat_ref[tm:tm + HALO, :] = halo.astype(BF16)
    hc = hcat_ref[...]
    sub = lax.broadcasted_iota(jnp.int32, (8, FF_TILE), 0)

    for f in range(D_FF // FF_TILE):
        halves = []
        for off in (f * FF_TILE, D_FF + f * FF_TILE):
            sl = slice(off, off + FF_TILE)
            a = jnp.dot(hc, wu_ref[:, sl], preferred_element_type=F32)
            cur = a[0:tm]
            edge = a[tm:tm + 8]
            dn = pltpu.roll(cur, 1, axis=0)
            up = pltpu.roll(cur, tm - 1, axis=0)
            prev = jnp.concatenate([jnp.where(sub == 0, edge, dn[0:8]), dn[8:]], axis=0)
            nxt = jnp.concatenate([up[:tm - 8], jnp.where(sub == 7, edge, up[tm - 8:])], axis=0)
            halves.append(prev * cw_ref[0:1, sl] + cur * cw_ref[1:2, sl] + nxt * cw_ref[2:3, sl] + cw_ref[3:4, sl])
        gate, val = halves
        act_ref[:, f * FF_TILE:(f + 1) * FF_TILE] = ((gate * jax.nn.sigmoid(gate)) * val).astype(BF16)
    acc = jnp.dot(act_ref[...], wd_ref[...], preferred_element_type=F32)
    g2 = mod_ref[0][:, 5 * d:6 * d]
    o_ref[0] = x_ref[0] + g2 * acc


def _conv_ffn(h2, x_mid, mod_l, w_up, cw, w_down, n_ctx_tiles):
    bsz, tt, d = x_mid.shape
    tm = ROW_TILE
    nt = tt // tm
    hb = tm // HALO
    n_halo_blocks = tt // HALO
    row = lambda b, t: (b, t, 0)
    const = lambda b, t: (0, 0)
    kern = functools.partial(_ffn_kernel, n_ctx_tiles=n_ctx_tiles)
    return pl.pallas_call(
        kern,
        grid=(bsz, nt),
        in_specs=[pl.BlockSpec((1, tm, d), row),
                  pl.BlockSpec((1, HALO, d), lambda b, t: (b, jnp.maximum(t * hb - 1, 0), 0)),
                  pl.BlockSpec((1, HALO, d), lambda b, t: (b, jnp.minimum((t + 1) * hb, n_halo_blocks - 1), 0)),
                  pl.BlockSpec((1, tm, d), row),
                  pl.BlockSpec((1, 1, 6 * d), lambda b, t: (jnp.where(t < n_ctx_tiles, bsz, b), 0, 0)),
                  pl.BlockSpec((d, 2 * D_FF), const),
                  pl.BlockSpec((8, 2 * D_FF), const),
                  pl.BlockSpec((D_FF, d), const)],
        out_specs=pl.BlockSpec((1, tm, d), row),
        out_shape=jax.ShapeDtypeStruct((bsz, tt, d), F32),
        scratch_shapes=[pltpu.VMEM((tm + HALO, d), BF16), pltpu.VMEM((tm, D_FF), BF16)],
        compiler_params=_cparams("parallel", "parallel"),
        name="conv_ffn",
    )(h2, h2, h2, x_mid, mod_l, w_up, cw, w_down)


def _rope_tables(s_len, t_ctx):
    nf = HEAD_DIM // 4
    rows = s_len // GRID_W
    pos = jnp.arange(s_len)
    row = jnp.repeat(jnp.arange(rows), GRID_W).astype(F32)
    colp = (pos % GRID_W).astype(F32)
    inv = ROPE_THETA ** (-jnp.arange(nf, dtype=F32) / nf)
    ang_r, ang_c = row[:, None] * inv[None], colp[:, None] * inv[None]
    cr, cc, sr, sc = jnp.cos(ang_r), jnp.cos(ang_c), jnp.sin(ang_r), jnp.sin(ang_c)
    z = jnp.zeros_like(sr)
    cos = jnp.concatenate([cr, cr, cc, cc], axis=1)
    sa = jnp.concatenate([-sr, z, -sc, z], axis=1)
    sb = jnp.concatenate([z, sr, z, sc], axis=1)
    pad = lambda a, fill: jnp.concatenate([jnp.full((t_ctx, HEAD_DIM), fill, F32), a], axis=0)
    two = lambda a: jnp.concatenate([a, a], axis=1)
    return two(pad(cos, 1.0)), two(pad(sa, 0.0)), two(pad(sb, 0.0))


def kernel(x, c, ctx, c_ctx, w_ada, b_ada, g_norm1, w_in, b_in, g_q, g_k, g_mh, g_v, w_sp, b_sp,
           w_out, g_norm2, w_up, conv_w, conv_b, w_down):
    bsz, s_len, d = x.shape
    t_ctx = ctx.shape[1]
    depth = w_ada.shape[0]
    tt = t_ctx + s_len
    assert d == D_MODEL and t_ctx % ML_CHUNK == 0 and s_len % ML_CHUNK == 0 and ROW_TILE == ML_CHUNK
    n_ctx_tiles = t_ctx // ROW_TILE

    o = ATTN_W + 2 * KV_W
    gate_off = o + 4 * ML_W
    cm_off = gate_off + N_GATES
    segs = ([(h * HEAD_DIM, (h + 1) * HEAD_DIM) for h in Q_HEAD_ORDER]
            + [(ATTN_W, gate_off), (cm_off, cm_off + 2 * CM_W), (gate_off, gate_off + N_GATES)])
    pad = N_PROJ - sum(e - s for s, e in segs)
    w_in_p = jnp.concatenate([w_in[:, :, s:e].astype(BF16) for s, e in segs]
                             + [jnp.zeros((depth, d, pad), BF16)], axis=2)
    b_in_p = jnp.concatenate([b_in[:, s:e] for s, e in segs] + [jnp.zeros((depth, pad), F32)], axis=1)[:, None, :]
    gqk = jnp.concatenate([jnp.tile(g_q, (1, N_Q_HEADS)) * (HEAD_DIM ** -0.5 * LOG2_E), jnp.tile(g_k, (1, N_KV_HEADS))],
                          axis=1)[:, None, :]
    w_out_p = jnp.concatenate([w_out[:, h * HEAD_DIM:(h + 1) * HEAD_DIM, :].astype(BF16) for h in Q_HEAD_ORDER]
                              + [w_out[:, ATTN_W:, :].astype(BF16)], axis=1)
    w_up_b = w_up.astype(BF16)
    w_down_b = w_down.astype(BF16)
    cw = jnp.concatenate([conv_w, conv_b[:, None, :], jnp.zeros((depth, 4, 2 * D_FF), F32)], axis=1)
    w_sp_b = w_sp.astype(BF16)
    b_sp_l = jnp.repeat(jnp.swapaxes(b_sp, 1, 2), HEAD_DIM, axis=2)
    blk = np.arange(LANES) // HEAD_DIM
    gones = jnp.asarray(blk[:, None] == blk[None, :], BF16)
    ii = np.arange(ML_CHUNK)
    tril = jnp.asarray(ii[None, :] <= ii[:, None], BF16)
    triu = jnp.asarray(ii[None, :] >= ii[:, None], BF16)
    gate_row = np.arange(N_GATES)[:, None]
    lane_head = (np.arange(ML_W) // HEAD_DIM)[None, :]
    esel = jnp.asarray(np.stack([gate_row == ML_HEADS + lane_head, gate_row == 3 * ML_HEADS + lane_head]), BF16)
    cos_t, sa_t, sb_t = _rope_tables(s_len, t_ctx)

    n_rows = -(-(bsz + 1) // 8) * 8
    cs = jnp.concatenate([c, c_ctx[None, :], jnp.zeros((n_rows - bsz - 1, d), F32)], axis=0)
    mod = _modulation(cs, w_ada, b_ada)

    x_all = (ctx, x)
    for l in range(depth):
        mod_l = mod[l][:, None, :]
        qk, v, mqkv, mo, cuv, gates_t = _in_proj(x_all, tt, mod_l, g_norm1[l][None], w_in_p[l], b_in_p[l], gqk[l],
                                                 gones, cos_t, sa_t, sb_t, n_ctx_tiles)
        attn = _attention(qk, v, t_ctx)
        ml = _mlstm(mqkv, gates_t, mo, g_mh[l][None], gones, esel, tril, triu, t_ctx)
        skip = n_ctx_tiles if l == depth - 1 else 0
        x_mid, h2 = _out_proj(attn, ml, cuv, x_all, mod_l, g_norm2[l][None], w_out_p[l], g_v[l][None], gones,
                              w_sp_b[l], b_sp_l[l], n_ctx_tiles, skip)
        x_all = _conv_ffn(h2, x_mid, mod_l, w_up_b[l], cw[l], w_down_b[l], n_ctx_tiles - skip)
    return x_all
```

```python
import functools

import jax
import jax.numpy as jnp
import numpy as np
from jax import lax
from jax.experimental import pallas as pl
from jax.experimental.pallas import tpu as pltpu

F32 = jnp.float32
BF16 = jnp.bfloat16

D_MODEL = 1024
HEAD_DIM = 64
N_Q_HEADS = 8
N_KV_HEADS = 2
GRID_W = 64
ROPE_THETA = 10000.0
EPS = 1e-6
ATTN_W = N_Q_HEADS * HEAD_DIM
KV_W = N_KV_HEADS * HEAD_DIM
ML_HEADS = 4
ML_W = ML_HEADS * HEAD_DIM
CM_GROUPS = 4
CM_W = CM_GROUPS * HEAD_DIM
CM_CHUNK = 128
N_GATES = 4 * ML_HEADS
D_FF = 2816
LANES = 128

OFF_Q, OFF_K, OFF_V = 0, 512, 640
OFF_MQ, OFF_MO, OFF_CU, OFF_GT = 768, 1536, 1792, 2304
N_PROJ = 2432
QK_W = ATTN_W + KV_W

ROW_TILE = 256
ML_CHUNK = 256
FF_TILE = 256
HALO = 16
ATTN_CHUNK = 128
LOG2_E = 1.4426950408889634
VMEM_LIMIT = 56 * 1024 * 1024

Q_HEAD_ORDER = (0, 4, 1, 5, 2, 6, 3, 7)


def _cparams(*sem):
    return pltpu.CompilerParams(dimension_semantics=sem, vmem_limit_bytes=VMEM_LIMIT)


def _group_mean_sq(x, g_ones):
    x2 = x * x
    hi = x2.astype(BF16)
    lo = (x2 - hi.astype(F32)).astype(BF16)
    g2 = jnp.concatenate([g_ones, g_ones], axis=0)
    cols = []
    for j in range(x.shape[1] // LANES):
        sl = slice(j * LANES, (j + 1) * LANES)
        cols.append(jnp.dot(jnp.concatenate([hi[:, sl], lo[:, sl]], axis=1), g2, preferred_element_type=F32))
    s = cols[0] if len(cols) == 1 else jnp.concatenate(cols, axis=1)
    return s * (1.0 / HEAD_DIM)


def _mod_kernel(c_ref, w_ref, b_ref, o_ref):
    c = c_ref[...]
    sc = c * jax.nn.sigmoid(c)
    o_ref[0] = jnp.dot(sc, w_ref[0], preferred_element_type=F32,
                       precision=lax.Precision.HIGHEST) + b_ref[0]


def _modulation(cs, w_ada, b_ada):
    depth, d, d6 = w_ada.shape
    r = cs.shape[0]
    nj = d6 // d
    return pl.pallas_call(
        _mod_kernel,
        grid=(depth, nj),
        in_specs=[pl.BlockSpec((r, d), lambda l, j: (0, 0)),
                  pl.BlockSpec((1, d, d), lambda l, j: (l, 0, j)),
                  pl.BlockSpec((1, 1, d), lambda l, j: (l, 0, j))],
        out_specs=pl.BlockSpec((1, r, d), lambda l, j: (l, 0, j)),
        out_shape=jax.ShapeDtypeStruct((depth, r, d6), F32),
        compiler_params=_cparams("arbitrary", "arbitrary"),
        name="adaln_mod",
    )(cs, w_ada, b_ada.reshape(depth, 1, d6))


def _layer_spec(stack, layer):
    return pl.BlockSpec((1,) + stack.shape[1:], lambda b, t: (layer,) + (0,) * (stack.ndim - 1))


def _stream_specs(stream, n_ctx_tiles, skip_tiles=0):
    d = D_MODEL
    if not isinstance(stream, tuple):
        return [stream], [pl.BlockSpec((1, ROW_TILE, d), lambda b, t: (b, t + skip_tiles, 0))]
    return list(stream), [
        pl.BlockSpec((1, ROW_TILE, d), lambda b, t: (b, jnp.minimum(t + skip_tiles, n_ctx_tiles - 1), 0)),
        pl.BlockSpec((1, ROW_TILE, d), lambda b, t: (b, jnp.maximum(t + skip_tiles - n_ctx_tiles, 0), 0))]


def _stream_tile(refs, n_ctx_tiles, skip_tiles=0):
    if len(refs) == 1:
        return refs[0][0]
    return jnp.where(pl.program_id(1) + skip_tiles < n_ctx_tiles, refs[0][0], refs[1][0])


def _in_kernel(*refs, n_stream, n_ctx_tiles):
    (mod_ref, g1_ref, w_ref, b_ref, gqk_ref, gones_ref, cos_ref, sa_ref, sb_ref,
     qk_ref, v_ref, mqkv_ref, mo_ref, cuv_ref, gt_ref) = refs[n_stream:]
    d = D_MODEL
    x = _stream_tile(refs[:n_stream], n_ctx_tiles)
    mod = mod_ref[0]
    sh1, s1 = mod[:, 0:d], mod[:, d:2 * d]
    ms = jnp.mean(x * x, axis=-1, keepdims=True)
    h = (x * lax.rsqrt(ms + EPS)) * g1_ref[...] * (1 + s1) + sh1
    hb = h.astype(BF16)
    p = jnp.concatenate([jnp.dot(hb, w_ref[0, :, 0:OFF_MQ], preferred_element_type=F32),
                         jnp.dot(hb, w_ref[0, :, OFF_MQ:], preferred_element_type=F32)], axis=1) + b_ref[...]

    qk = p[:, OFF_Q:OFF_Q + QK_W]
    gms = _group_mean_sq(qk, gones_ref[...])
    qn = (qk * lax.rsqrt(gms + EPS)) * gqk_ref[...]
    cos, sa, sb = cos_ref[...], sa_ref[...], sb_ref[...]
    for j in range(QK_W // LANES):
        xb = qn[:, j * LANES:(j + 1) * LANES]
        up = pltpu.roll(xb, LANES - 16, axis=1)
        dn = pltpu.roll(xb, 16, axis=1)
        qk_ref[0, :, j * LANES:(j + 1) * LANES] = (xb * cos + up * sa + dn * sb).astype(BF16)

    v_ref[0] = p[:, OFF_V:OFF_V + KV_W].T.astype(BF16)
    mqkv_ref[0, :, 0:ML_W] = p[:, OFF_MQ:OFF_MQ + ML_W].astype(BF16)
    mqkv_ref[0, :, ML_W:2 * ML_W] = (p[:, OFF_MQ + ML_W:OFF_MQ + 2 * ML_W] * (HEAD_DIM ** -0.5)).astype(BF16)
    mqkv_ref[0, :, 2 * ML_W:3 * ML_W] = p[:, OFF_MQ + 2 * ML_W:OFF_MQ + 3 * ML_W].astype(BF16)
    mo_ref[0] = p[:, OFF_MO:OFF_MO + ML_W]
    cuv_ref[0] = p[:, OFF_CU:OFF_CU + 2 * CM_W]
    gt_ref[0, 0] = p[:, OFF_GT:OFF_GT + LANES].T[0:N_GATES, :]


def _in_proj(stream, tt, mod_l, g1, w_in, layer, b_in, gqk, gones, cos_t, sa_t, sb_t, n_ctx_tiles):
    x_arrays, x_specs = _stream_specs(stream, n_ctx_tiles)
    bsz, d = x_arrays[0].shape[0], D_MODEL
    tm = ROW_TILE
    nt = tt // tm
    row = lambda b, t: (b, t, 0)
    const = lambda b, t: (0, 0)
    tab = lambda b, t: (t, 0)
    outs = [(QK_W, BF16), (KV_W, BF16), (3 * ML_W, BF16), (ML_W, F32), (2 * CM_W, F32), (LANES, F32)]
    return pl.pallas_call(
        functools.partial(_in_kernel, n_stream=len(x_arrays), n_ctx_tiles=n_ctx_tiles),
        grid=(bsz, nt),
        in_specs=x_specs + [
                  pl.BlockSpec((1, 1, 6 * d), lambda b, t: (jnp.where(t < n_ctx_tiles, bsz, b), 0, 0)),
                  pl.BlockSpec((1, d), const),
                  _layer_spec(w_in, layer),
                  pl.BlockSpec((1, N_PROJ), const),
                  pl.BlockSpec((1, QK_W), const),
                  pl.BlockSpec((LANES, LANES), const),
                  pl.BlockSpec((tm, LANES), tab),
                  pl.BlockSpec((tm, LANES), tab),
                  pl.BlockSpec((tm, LANES), tab)],
        out_specs=[pl.BlockSpec((1, KV_W, tm), lambda b, t: (b, 0, t)) if i == 1
                   else pl.BlockSpec((1, 1, N_GATES, tm), lambda b, t: (b, t, 0, 0)) if i == 5
                   else pl.BlockSpec((1, tm, w), row) for i, (w, _) in enumerate(outs)],
        out_shape=[jax.ShapeDtypeStruct((bsz, KV_W, tt) if i == 1 else (bsz, nt, N_GATES, tm) if i == 5
                                        else (bsz, tt, w), dt) for i, (w, dt) in enumerate(outs)],
        compiler_params=_cparams("parallel", "parallel"),
        name="in_proj",
    )(*x_arrays, mod_l, g1, w_in, b_in, gqk, gones, cos_t, sa_t, sb_t)


def _attn_kernel(q_ref, k_ref, vt_ref, o_ref, s_ref, e_ref, *, n_ctx_tiles, t_ctx):
    t = pl.program_id(1)
    tq = q_ref.shape[1]
    low = lax.broadcasted_iota(jnp.int32, (tq, LANES), 1) < HEAD_DIM
    n_pairs = ATTN_W // LANES
    groups = [(kvh, pp) for pp in range(n_pairs // 2) for kvh in range(2)]

    def attend(kv_len):
        def scores(g):
            k = k_ref[0, 0:kv_len, :]
            kvh, pp = groups[g]
            sel = low if kvh == 0 else jnp.logical_not(low)
            q2 = jnp.concatenate(
                [jnp.where(sel, q_ref[0, :, p * LANES:(p + 1) * LANES], jnp.zeros((tq, LANES), BF16))
                 for p in (2 * pp, 2 * pp + 1)], axis=0)
            s_ref[g % 2, 0:kv_len, :] = lax.dot_general(k, q2, (((1,), (1,)), ((), ())),
                                                        preferred_element_type=F32)

        scores(0)
        outs = {}
        for g, (kvh, pp) in enumerate(groups):
            if g + 1 < len(groups):
                scores(g + 1)
            chunks = [slice(c, c + ATTN_CHUNK) for c in range(0, kv_len, ATTN_CHUNK)]
            fold = lambda a: a.reshape(ATTN_CHUNK // 8, 8, 2 * tq)
            m8 = None
            for rows in chunks:
                mc = jnp.max(fold(s_ref[g % 2, rows, :]), axis=0)
                m8 = mc if m8 is None else jnp.maximum(m8, mc)
            m = jnp.max(m8, axis=0, keepdims=True)
            for rows in chunks:
                e_ref[g % 2, rows, :] = jnp.exp2(s_ref[g % 2, rows, :] - m).astype(BF16)
            vt = vt_ref[0, kvh * HEAD_DIM:(kvh + 1) * HEAD_DIM, 0:kv_len]
            vt1 = jnp.concatenate([vt, jnp.ones((16, kv_len), BF16)], axis=0)
            ov = jnp.dot(vt1, e_ref[g % 2, 0:kv_len, :], preferred_element_type=F32)
            outs[kvh] = ov[0:HEAD_DIM] / ov[HEAD_DIM:HEAD_DIM + 1]
            if kvh == 1:
                for j, p in enumerate((2 * pp, 2 * pp + 1)):
                    both = jnp.concatenate([outs[0][:, j * tq:(j + 1) * tq], outs[1][:, j * tq:(j + 1) * tq]], axis=0)
                    o_ref[0, :, p * LANES:(p + 1) * LANES] = both.T.astype(BF16)

    @pl.when(t < n_ctx_tiles)
    def _():
        attend(t_ctx)

    @pl.when(t >= n_ctx_tiles)
    def _():
        attend(k_ref.shape[1])


def _attention(qk, vt, t_ctx):
    bsz, tt, _ = qk.shape
    tq = ROW_TILE
    kern = functools.partial(_attn_kernel, n_ctx_tiles=t_ctx // tq, t_ctx=t_ctx)
    return pl.pallas_call(
        kern,
        grid=(bsz, tt // tq),
        in_specs=[pl.BlockSpec((1, tq, ATTN_W), lambda b, t: (b, t, 0)),
                  pl.BlockSpec((1, tt, KV_W), lambda b, t: (b, 0, ATTN_W // KV_W)),
                  pl.BlockSpec((1, KV_W, tt), lambda b, t: (b, 0, 0))],
        out_specs=pl.BlockSpec((1, tq, ATTN_W), lambda b, t: (b, t, 0)),
        out_shape=jax.ShapeDtypeStruct((bsz, tt, ATTN_W), BF16),
        scratch_shapes=[pltpu.VMEM((2, tt, 2 * tq), F32), pltpu.VMEM((2, tt, 2 * tq), BF16)],
        compiler_params=_cparams("parallel", "parallel"),
        name="attention",
    )(qk, qk, vt)


def _log_sigmoid(x):
    return jnp.minimum(x, 0.0) - jnp.log1p(jnp.exp(-jnp.abs(x)))


def _split3(a):
    hi = a.astype(BF16)
    r1 = a - hi.astype(F32)
    mid = r1.astype(BF16)
    lo = (r1 - mid.astype(F32)).astype(BF16)
    return hi, mid, lo


def _mlstm_kernel(qkv_ref, gt_ref, mo_ref, gmh_ref, gones_ref, esel_ref, tril_ref, triu_ref,
                  o_ref, hf_ref, hb_ref, cn_ref, m_ref, neg_ref, cmask_ref, *, n_ctx_chunks):
    L = ML_CHUNK
    W = ML_W
    n_chunks = qkv_ref.shape[1] // L

    row = lax.broadcasted_iota(jnp.int32, (L, L), 0)
    col = lax.broadcasted_iota(jnp.int32, (L, L), 1)
    neg_ref[0] = jnp.where(col <= row, 0.0, -jnp.inf)
    neg_ref[1] = jnp.where(col >= row, 0.0, -jnp.inf)
    blk_r = lax.broadcasted_iota(jnp.int32, (W, 2 * W), 0) // HEAD_DIM
    blk_c = (lax.broadcasted_iota(jnp.int32, (W, 2 * W), 1) % W) // HEAD_DIM
    cmask_ref[...] = (blk_r == blk_c).astype(F32)
    low = lax.broadcasted_iota(jnp.int32, (L, LANES), 1) < HEAD_DIM
    ones_blk = jnp.ones((L, W), BF16)
    cn_ref[...] = jnp.zeros_like(cn_ref)
    m_ref[...] = jnp.zeros_like(m_ref)

    def step(rev, c):
        r0 = pl.multiple_of(c * L, L)
        q = qkv_ref[0, pl.ds(r0, L), 0:W]
        k = qkv_ref[0, pl.ds(r0, L), W:2 * W]
        v = qkv_ref[0, pl.ds(r0, L), 2 * W:3 * W]
        gt = gt_ref[0, c]
        goff = 2 * ML_HEADS if rev else 0
        tri_r = tril_ref[...] if rev else triu_ref[...]
        last = 0 if rev else L - 1

        brows = sum(jnp.dot(part, tri_r, preferred_element_type=F32) for part in _split3(_log_sigmoid(gt)))
        bexp = sum(lax.dot_general(part, esel_ref[rev], (((0,), (0,)), ((), ())), preferred_element_type=F32)
                   for part in _split3(brows))
        r_rows = gt[goff:goff + ML_HEADS] - brows[goff + ML_HEADS:goff + 2 * ML_HEADS]

        cn = cn_ref[rev]
        m_prev = m_ref[rev][0:1, :]
        qc = jnp.dot(q, cn.astype(BF16), preferred_element_type=F32)
        kt = k.astype(F32).T
        kt_b = kt.astype(BF16)

        mus, asums, avs, gsr = [], [], [], []
        for h in range(ML_HEADS):
            cols = slice((h // 2) * LANES, (h // 2 + 1) * LANES)
            sel = low if h % 2 == 0 else jnp.logical_not(low)
            qh = jnp.where(sel, q[:, cols], jnp.zeros((L, LANES), BF16))
            s = jnp.dot(qh, kt_b[cols, :], preferred_element_type=F32)
            r = r_rows[h:h + 1, :]
            rm = r + neg_ref[rev]
            mu = jnp.maximum(jnp.max(rm, axis=-1, keepdims=True), m_prev[:, h * HEAD_DIM:h * HEAD_DIM + 1])
            a = jnp.exp(rm - mu) * s
            asums.append(jnp.sum(a, axis=-1, keepdims=True))
            avs.append(jnp.dot(a.astype(BF16), v[:, cols], preferred_element_type=F32))
            mus.append(mu)
            gsr.append(jnp.broadcast_to(jnp.exp(r - mu[last:last + 1, :]), (HEAD_DIM, L)))

        pair = lambda xs: jnp.concatenate([jnp.where(low, xs[0], xs[1]), jnp.where(low, xs[2], xs[3])], axis=1)
        mu_x, asum_x, av_x = pair(mus), pair(asums), pair(avs)
        w_inter = jnp.exp(m_prev - mu_x)
        m_t = bexp + mu_x
        num = w_inter * qc[:, 0:W] + av_x
        den = w_inter * qc[:, W:2 * W] + asum_x
        dst = hb_ref if rev else hf_ref
        dst[pl.ds(r0, L), :] = num / jnp.maximum(jnp.abs(den), jnp.exp(-m_t))

        mu_last = mu_x[last:last + 1, :]
        wc = jnp.exp(m_prev - mu_last)
        gkt = (kt * jnp.concatenate(gsr, axis=0)).astype(BF16)
        upd = jnp.dot(gkt, jnp.concatenate([v, ones_blk], axis=1), preferred_element_type=F32)
        cn_ref[rev] = jnp.concatenate([wc, wc], axis=1) * cn + upd * cmask_ref[...]
        m_ref[rev] = jnp.broadcast_to(bexp[last:last + 1, :] + mu_last, (8, W))

    def body(j, carry):
        step(0, j)
        cb = jnp.where(j < n_ctx_chunks, n_ctx_chunks - 1 - j, n_chunks - 1 - (j - n_ctx_chunks))
        step(1, cb)
        return carry

    lax.fori_loop(0, n_chunks, body, 0)

    def merge(c, carry):
        r0 = pl.multiple_of(c * L, L)
        hs = hf_ref[pl.ds(r0, L), :] + hb_ref[pl.ds(r0, L), :]
        gms = _group_mean_sq(hs, gones_ref[...])
        y = (hs * lax.rsqrt(gms + EPS)) * gmh_ref[...]
        o_ref[0, pl.ds(r0, L), :] = (jax.nn.sigmoid(mo_ref[0, pl.ds(r0, L), :]) * y).astype(BF16)
        return carry

    lax.fori_loop(0, n_chunks, merge, 0)


def _mlstm(mqkv, gates_t, mo, g_mh, gones, esel, tril, triu, t_ctx):
    bsz, tt, _ = mqkv.shape
    L = ML_CHUNK
    nc = tt // L
    kern = functools.partial(_mlstm_kernel, n_ctx_chunks=t_ctx // L)
    per_b = lambda b: (b, 0, 0)
    const = lambda b: (0, 0)
    return pl.pallas_call(
        kern,
        grid=(bsz,),
        in_specs=[pl.BlockSpec((1, tt, 3 * ML_W), per_b),
                  pl.BlockSpec((1, nc, N_GATES, L), lambda b: (b, 0, 0, 0)),
                  pl.BlockSpec((1, tt, ML_W), per_b),
                  pl.BlockSpec((1, ML_W), const),
                  pl.BlockSpec((LANES, LANES), const),
                  pl.BlockSpec((2, N_GATES, ML_W), lambda b: (0, 0, 0)),
                  pl.BlockSpec((L, L), const),
                  pl.BlockSpec((L, L), const)],
        out_specs=pl.BlockSpec((1, tt, ML_W), per_b),
        out_shape=jax.ShapeDtypeStruct((bsz, tt, ML_W), BF16),
        scratch_shapes=[pltpu.VMEM((tt, ML_W), F32), pltpu.VMEM((tt, ML_W), F32),
                        pltpu.VMEM((2, ML_W, 2 * ML_W), F32), pltpu.VMEM((2, 8, ML_W), F32),
                        pltpu.VMEM((2, L, L), F32), pltpu.VMEM((ML_W, 2 * ML_W), F32)],
        compiler_params=_cparams("parallel"),
        name="mlstm",
    )(mqkv, gates_t, mo, g_mh, gones, esel, tril, triu)


def _chunk_mlp(uv, g_v, g_ones, wsp_ref, b_sp):
    lane = lax.broadcasted_iota(jnp.int32, (CM_CHUNK, CM_W), 1)
    u = jax.nn.gelu(uv[:, 0:CM_W])
    v = jax.nn.gelu(uv[:, CM_W:2 * CM_W])
    gms = _group_mean_sq(v, g_ones)
    vb = ((v * lax.rsqrt(gms + EPS)) * g_v).astype(BF16)
    z = jnp.zeros((CM_CHUNK, CM_W), F32)
    for g in range(CM_GROUPS):
        zg = jnp.dot(wsp_ref[g], vb, preferred_element_type=F32)
        z = jnp.where((lane // HEAD_DIM) == g, zg, z)
    return (u * (z + b_sp)).astype(BF16)


def _out_kernel(*refs, n_stream, n_ctx_tiles, skip_tiles):
    (at_ref, ml_ref, uv_ref, mod_ref, g2_ref, w_ref, gv_ref, gones_ref, wsp_ref, bsp_ref,
     xo_ref, h2_ref) = refs[n_stream:]
    d = D_MODEL
    tm = at_ref.shape[1]
    mod = mod_ref[0]
    g1, sh2, s2 = mod[:, 2 * d:3 * d], mod[:, 3 * d:4 * d], mod[:, 4 * d:5 * d]
    cm = jnp.concatenate([_chunk_mlp(uv_ref[0, c:c + CM_CHUNK, :], gv_ref[...], gones_ref[...], wsp_ref, bsp_ref[...])
                          for c in range(0, tm, CM_CHUNK)], axis=0)
    mm = (jnp.dot(at_ref[0], w_ref[0, 0:ATTN_W, :], preferred_element_type=F32)
          + jnp.dot(ml_ref[0], w_ref[0, ATTN_W:ATTN_W + ML_W, :], preferred_element_type=F32)
          + jnp.dot(cm, w_ref[0, ATTN_W + ML_W:, :], preferred_element_type=F32))
    y = _stream_tile(refs[:n_stream], n_ctx_tiles, skip_tiles) + g1 * mm
    xo_ref[0] = y
    ms = jnp.mean(y * y, axis=-1, keepdims=True)
    h2_ref[0] = ((y * lax.rsqrt(ms + EPS)) * g2_ref[...] * (1 + s2) + sh2).astype(BF16)


def _out_proj(attn, ml, cuv, stream, mod_l, g2, w_out, layer, g_v, gones, w_sp, b_sp_lanes, n_ctx_tiles, skip_tiles):
    x_arrays, x_specs = _stream_specs(stream, n_ctx_tiles, skip_tiles)
    bsz, tt, d = attn.shape[0], attn.shape[1], D_MODEL
    tm = ROW_TILE
    t_out = tt - skip_tiles * tm
    row = lambda b, t: (b, t + skip_tiles, 0)
    out_row = lambda b, t: (b, t, 0)
    const = lambda b, t: (0, 0)
    return pl.pallas_call(
        functools.partial(_out_kernel, n_stream=len(x_arrays), n_ctx_tiles=n_ctx_tiles, skip_tiles=skip_tiles),
        grid=(bsz, t_out // tm),
        in_specs=x_specs + [
                  pl.BlockSpec((1, tm, ATTN_W), row),
                  pl.BlockSpec((1, tm, ML_W), row),
                  pl.BlockSpec((1, tm, 2 * CM_W), row),
                  pl.BlockSpec((1, 1, 6 * d), lambda b, t: (jnp.where(t + skip_tiles < n_ctx_tiles, bsz, b), 0, 0)),
                  pl.BlockSpec((1, d), const),
                  _layer_spec(w_out, layer),
                  pl.BlockSpec((1, CM_W), const),
                  pl.BlockSpec((LANES, LANES), const),
                  pl.BlockSpec((CM_GROUPS, CM_CHUNK, CM_CHUNK), lambda b, t: (0, 0, 0)),
                  pl.BlockSpec((CM_CHUNK, CM_W), const)],
        out_specs=[pl.BlockSpec((1, tm, d), out_row), pl.BlockSpec((1, tm, d), out_row)],
        out_shape=[jax.ShapeDtypeStruct((bsz, t_out, d), F32), jax.ShapeDtypeStruct((bsz, t_out, d), BF16)],
        compiler_params=_cparams("parallel", "parallel"),
        name="out_proj",
    )(*x_arrays, attn, ml, cuv, mod_l, g2, w_out, g_v, gones, w_sp, b_sp_lanes)


def _ffn_kernel(h_ref, hp_ref, hn_ref, x_ref, mod_ref, wu_ref, cw_ref, wd_ref, o_ref, hcat_ref, act_ref, *,
                n_ctx_tiles):
    d = D_MODEL
    tm = h_ref.shape[1]
    t = pl.program_id(1)
    nt = pl.num_programs(1)
    has_prev = jnp.logical_and(t != 0, t != n_ctx_tiles)
    has_next = jnp.logical_and(t != n_ctx_tiles - 1, t != nt - 1)
    hrow = lax.broadcasted_iota(jnp.int32, (HALO, d), 0)
    before = pltpu.roll(hp_ref[0].astype(F32), 1, axis=0)
    after = pltpu.roll(hn_ref[0].astype(F32), 7, axis=0)
    halo = jnp.where(jnp.logical_and(hrow == 0, has_prev), before,
                     jnp.where(jnp.logical_and(hrow == 7, has_next), after, 0.0))
    hcat_ref[0:tm, :] = h_ref[0]
    hcat_ref[tm:tm + HALO, :] = halo.astype(BF16)
    hc = hcat_ref[...]
    sub = lax.broadcasted_iota(jnp.int32, (8, FF_TILE), 0)

    for f in range(D_FF // FF_TILE):
        halves = []
        for off in (f * FF_TILE, D_FF + f * FF_TILE):
            sl = slice(off, off + FF_TILE)
            a = jnp.dot(hc, wu_ref[0, :, sl], preferred_element_type=F32)
            cur = a[0:tm]
            edge = a[tm:tm + 8]
            dn = pltpu.roll(cur, 1, axis=0)
            up = pltpu.roll(cur, tm - 1, axis=0)
            prev = jnp.concatenate([jnp.where(sub == 0, edge, dn[0:8]), dn[8:]], axis=0)
            nxt = jnp.concatenate([up[:tm - 8], jnp.where(sub == 7, edge, up[tm - 8:])], axis=0)
            halves.append(prev * cw_ref[0:1, sl] + cur * cw_ref[1:2, sl] + nxt * cw_ref[2:3, sl] + cw_ref[3:4, sl])
        gate, val = halves
        act_ref[:, f * FF_TILE:(f + 1) * FF_TILE] = ((gate * jax.nn.sigmoid(gate)) * val).astype(BF16)
    acc = jnp.dot(act_ref[...], wd_ref[0], preferred_element_type=F32)
    g2 = mod_ref[0][:, 5 * d:6 * d]
    o_ref[0] = x_ref[0] + g2 * acc


def _conv_ffn(h2, x_mid, mod_l, w_up, cw, w_down, layer, n_ctx_tiles):
    bsz, tt, d = x_mid.shape
    tm = ROW_TILE
    nt = tt // tm
    hb = tm // HALO
    n_halo_blocks = tt // HALO
    row = lambda b, t: (b, t, 0)
    const = lambda b, t: (0, 0)
    kern = functools.partial(_ffn_kernel, n_ctx_tiles=n_ctx_tiles)
    return pl.pallas_call(
        kern,
        grid=(bsz, nt),
        in_specs=[pl.BlockSpec((1, tm, d), row),
                  pl.BlockSpec((1, HALO, d), lambda b, t: (b, jnp.maximum(t * hb - 1, 0), 0)),
                  pl.BlockSpec((1, HALO, d), lambda b, t: (b, jnp.minimum((t + 1) * hb, n_halo_blocks - 1), 0)),
                  pl.BlockSpec((1, tm, d), row),
                  pl.BlockSpec((1, 1, 6 * d), lambda b, t: (jnp.where(t < n_ctx_tiles, bsz, b), 0, 0)),
                  _layer_spec(w_up, layer),
                  pl.BlockSpec((8, 2 * D_FF), const),
                  _layer_spec(w_down, layer)],
        out_specs=pl.BlockSpec((1, tm, d), row),
        out_shape=jax.ShapeDtypeStruct((bsz, tt, d), F32),
        scratch_shapes=[pltpu.VMEM((tm + HALO, d), BF16), pltpu.VMEM((tm, D_FF), BF16)],
        compiler_params=_cparams("parallel", "parallel"),
        name="conv_ffn",
    )(h2, h2, h2, x_mid, mod_l, w_up, cw, w_down)


def _rope_tables(s_len, t_ctx):
    nf = HEAD_DIM // 4
    rows = s_len // GRID_W
    pos = jnp.arange(s_len)
    row = jnp.repeat(jnp.arange(rows), GRID_W).astype(F32)
    colp = (pos % GRID_W).astype(F32)
    inv = ROPE_THETA ** (-jnp.arange(nf, dtype=F32) / nf)
    ang_r, ang_c = row[:, None] * inv[None], colp[:, None] * inv[None]
    cr, cc, sr, sc = jnp.cos(ang_r), jnp.cos(ang_c), jnp.sin(ang_r), jnp.sin(ang_c)
    z = jnp.zeros_like(sr)
    cos = jnp.concatenate([cr, cr, cc, cc], axis=1)
    sa = jnp.concatenate([-sr, z, -sc, z], axis=1)
    sb = jnp.concatenate([z, sr, z, sc], axis=1)
    pad = lambda a, fill: jnp.concatenate([jnp.full((t_ctx, HEAD_DIM), fill, F32), a], axis=0)
    two = lambda a: jnp.concatenate([a, a], axis=1)
    return two(pad(cos, 1.0)), two(pad(sa, 0.0)), two(pad(sb, 0.0))


def kernel(x, c, ctx, c_ctx, w_ada, b_ada, g_norm1, w_in, b_in, g_q, g_k, g_mh, g_v, w_sp, b_sp,
           w_out, g_norm2, w_up, conv_w, conv_b, w_down):
    bsz, s_len, d = x.shape
    t_ctx = ctx.shape[1]
    depth = w_ada.shape[0]
    tt = t_ctx + s_len
    assert d == D_MODEL and t_ctx % ML_CHUNK == 0 and s_len % ML_CHUNK == 0 and ROW_TILE == ML_CHUNK
    n_ctx_tiles = t_ctx // ROW_TILE

    o = ATTN_W + 2 * KV_W
    gate_off = o + 4 * ML_W
    cm_off = gate_off + N_GATES
    segs = ([(h * HEAD_DIM, (h + 1) * HEAD_DIM) for h in Q_HEAD_ORDER]
            + [(ATTN_W, gate_off), (cm_off, cm_off + 2 * CM_W), (gate_off, gate_off + N_GATES)])
    pad = N_PROJ - sum(e - s for s, e in segs)
    w_in_p = jnp.concatenate([w_in[:, :, s:e].astype(BF16) for s, e in segs]
                             + [jnp.zeros((depth, d, pad), BF16)], axis=2)
    b_in_p = jnp.concatenate([b_in[:, s:e] for s, e in segs] + [jnp.zeros((depth, pad), F32)], axis=1)[:, None, :]
    gqk = jnp.concatenate([jnp.tile(g_q, (1, N_Q_HEADS)) * (HEAD_DIM ** -0.5 * LOG2_E), jnp.tile(g_k, (1, N_KV_HEADS))],
                          axis=1)[:, None, :]
    w_out_p = jnp.concatenate([w_out[:, h * HEAD_DIM:(h + 1) * HEAD_DIM, :].astype(BF16) for h in Q_HEAD_ORDER]
                              + [w_out[:, ATTN_W:, :].astype(BF16)], axis=1)
    w_up_b = w_up.astype(BF16)
    w_down_b = w_down.astype(BF16)
    cw = jnp.concatenate([conv_w, conv_b[:, None, :], jnp.zeros((depth, 4, 2 * D_FF), F32)], axis=1)
    w_sp_b = w_sp.astype(BF16)
    b_sp_l = jnp.repeat(jnp.swapaxes(b_sp, 1, 2), HEAD_DIM, axis=2)
    blk = np.arange(LANES) // HEAD_DIM
    gones = jnp.asarray(blk[:, None] == blk[None, :], BF16)
    ii = np.arange(ML_CHUNK)
    tril = jnp.asarray(ii[None, :] <= ii[:, None], BF16)
    triu = jnp.asarray(ii[None, :] >= ii[:, None], BF16)
    gate_row = np.arange(N_GATES)[:, None]
    lane_head = (np.arange(ML_W) // HEAD_DIM)[None, :]
    esel = jnp.asarray(np.stack([gate_row == ML_HEADS + lane_head, gate_row == 3 * ML_HEADS + lane_head]), BF16)
    cos_t, sa_t, sb_t = _rope_tables(s_len, t_ctx)

    n_rows = -(-(bsz + 1) // 8) * 8
    cs = jnp.concatenate([c, c_ctx[None, :], jnp.zeros((n_rows - bsz - 1, d), F32)], axis=0)
    mod = _modulation(cs, w_ada, b_ada)

    x_all = (ctx, x)
    for l in range(depth):
        mod_l = mod[l][:, None, :]
        qk, v, mqkv, mo, cuv, gates_t = _in_proj(x_all, tt, mod_l, g_norm1[l][None], w_in_p, l, b_in_p[l], gqk[l],
                                                 gones, cos_t, sa_t, sb_t, n_ctx_tiles)
        attn = _attention(qk, v, t_ctx)
        ml = _mlstm(mqkv, gates_t, mo, g_mh[l][None], gones, esel, tril, triu, t_ctx)
        skip = n_ctx_tiles if l == depth - 1 else 0
        x_mid, h2 = _out_proj(attn, ml, cuv, x_all, mod_l, g_norm2[l][None], w_out_p, l, g_v[l][None], gones,
                              w_sp_b[l], b_sp_l[l], n_ctx_tiles, skip)
        x_all = _conv_ffn(h2, x_mid, mod_l, w_up_b, cw[l], w_down_b, l, n_ctx_tiles - skip)
    return x_all
```

```python
import functools

import jax
import jax.numpy as jnp
import numpy as np
from jax import lax
from jax.experimental import pallas as pl
from jax.experimental.pallas import tpu as pltpu

F32 = jnp.float32
BF16 = jnp.bfloat16

D_MODEL = 1024
HEAD_DIM = 64
N_Q_HEADS = 8
N_KV_HEADS = 2
GRID_W = 64
ROPE_THETA = 10000.0
EPS = 1e-6
ATTN_W = N_Q_HEADS * HEAD_DIM
KV_W = N_KV_HEADS * HEAD_DIM
ML_HEADS = 4
ML_W = ML_HEADS * HEAD_DIM
CM_GROUPS = 4
CM_W = CM_GROUPS * HEAD_DIM
CM_CHUNK = 128
N_GATES = 4 * ML_HEADS
D_FF = 2816
LANES = 128

OFF_Q, OFF_K, OFF_V = 0, 512, 640
OFF_MQ, OFF_MO, OFF_CU, OFF_GT = 768, 1536, 1792, 2304
N_PROJ = 2432
QK_W = ATTN_W + KV_W

ROW_TILE = 256
ML_CHUNK = 256
FF_TILE = 256
HALO = 16
ATTN_CHUNK = 128
LOG2_E = 1.4426950408889634
VMEM_LIMIT = 56 * 1024 * 1024

Q_HEAD_ORDER = (0, 4, 1, 5, 2, 6, 3, 7)


def _cparams(*sem):
    return pltpu.CompilerParams(dimension_semantics=sem, vmem_limit_bytes=VMEM_LIMIT)


def _group_mean_sq(x, g_ones):
    x2 = x * x
    hi = x2.astype(BF16)
    lo = (x2 - hi.astype(F32)).astype(BF16)
    g2 = jnp.concatenate([g_ones, g_ones], axis=0)
    cols = []
    for j in range(x.shape[1] // LANES):
        sl = slice(j * LANES, (j + 1) * LANES)
        cols.append(jnp.dot(jnp.concatenate([hi[:, sl], lo[:, sl]], axis=1), g2, preferred_element_type=F32))
    s = cols[0] if len(cols) == 1 else jnp.concatenate(cols, axis=1)
    return s * (1.0 / HEAD_DIM)


def _mod_kernel(c_ref, w_ref, b_ref, o_ref):
    c = c_ref[...]
    sc = c * jax.nn.sigmoid(c)
    o_ref[0] = jnp.dot(sc, w_ref[0], preferred_element_type=F32,
                       precision=lax.Precision.HIGHEST) + b_ref[0]


def _modulation(cs, w_ada, b_ada):
    depth, d, d6 = w_ada.shape
    r = cs.shape[0]
    nj = d6 // d
    return pl.pallas_call(
        _mod_kernel,
        grid=(depth, nj),
        in_specs=[pl.BlockSpec((r, d), lambda l, j: (0, 0)),
                  pl.BlockSpec((1, d, d), lambda l, j: (l, 0, j)),
                  pl.BlockSpec((1, 1, d), lambda l, j: (l, 0, j))],
        out_specs=pl.BlockSpec((1, r, d), lambda l, j: (l, 0, j)),
        out_shape=jax.ShapeDtypeStruct((depth, r, d6), F32),
        compiler_params=_cparams("arbitrary", "arbitrary"),
        name="adaln_mod",
    )(cs, w_ada, b_ada.reshape(depth, 1, d6))


def _layer_spec(stack, layer):
    return pl.BlockSpec((1,) + stack.shape[1:], lambda b, t: (layer,) + (0,) * (stack.ndim - 1))


def _stream_specs(stream, n_ctx_tiles, skip_tiles=0):
    d = D_MODEL
    if not isinstance(stream, tuple):
        return [stream], [pl.BlockSpec((1, ROW_TILE, d), lambda b, t: (b, t + skip_tiles, 0))]
    return list(stream), [
        pl.BlockSpec((1, ROW_TILE, d), lambda b, t: (b, jnp.minimum(t + skip_tiles, n_ctx_tiles - 1), 0)),
        pl.BlockSpec((1, ROW_TILE, d), lambda b, t: (b, jnp.maximum(t + skip_tiles - n_ctx_tiles, 0), 0))]


def _stream_tile(refs, n_ctx_tiles, skip_tiles=0):
    if len(refs) == 1:
        return refs[0][0]
    return jnp.where(pl.program_id(1) + skip_tiles < n_ctx_tiles, refs[0][0], refs[1][0])


def _in_kernel(*refs, n_stream, n_ctx_tiles):
    (mod_ref, g1_ref, w_ref, b_ref, gqk_ref, gones_ref, cos_ref, sa_ref, sb_ref,
     qk_ref, v_ref, mqkv_ref, mo_ref, cuv_ref, gt_ref) = refs[n_stream:]
    d = D_MODEL
    x = _stream_tile(refs[:n_stream], n_ctx_tiles)
    mod = mod_ref[0]
    sh1, s1 = mod[:, 0:d], mod[:, d:2 * d]
    ms = jnp.mean(x * x, axis=-1, keepdims=True)
    h = (x * lax.rsqrt(ms + EPS)) * g1_ref[...] * (1 + s1) + sh1
    hb = h.astype(BF16)
    p = jnp.concatenate([jnp.dot(hb, w_ref[0, :, 0:OFF_MQ], preferred_element_type=F32),
                         jnp.dot(hb, w_ref[0, :, OFF_MQ:], preferred_element_type=F32)], axis=1) + b_ref[...]

    qk = p[:, OFF_Q:OFF_Q + QK_W]
    gms = _group_mean_sq(qk, gones_ref[...])
    qn = (qk * lax.rsqrt(gms + EPS)) * gqk_ref[...]
    cos, sa, sb = cos_ref[...], sa_ref[...], sb_ref[...]
    for j in range(QK_W // LANES):
        xb = qn[:, j * LANES:(j + 1) * LANES]
        up = pltpu.roll(xb, LANES - 16, axis=1)
        dn = pltpu.roll(xb, 16, axis=1)
        qk_ref[0, :, j * LANES:(j + 1) * LANES] = (xb * cos + up * sa + dn * sb).astype(BF16)

    v_ref[0] = p[:, OFF_V:OFF_V + KV_W].T.astype(BF16)
    mqkv_ref[0, :, 0:ML_W] = p[:, OFF_MQ:OFF_MQ + ML_W].astype(BF16)
    mqkv_ref[0, :, ML_W:2 * ML_W] = (p[:, OFF_MQ + ML_W:OFF_MQ + 2 * ML_W] * (HEAD_DIM ** -0.5)).astype(BF16)
    mqkv_ref[0, :, 2 * ML_W:3 * ML_W] = p[:, OFF_MQ + 2 * ML_W:OFF_MQ + 3 * ML_W].astype(BF16)
    mo_ref[0] = p[:, OFF_MO:OFF_MO + ML_W]
    cuv_ref[0] = p[:, OFF_CU:OFF_CU + 2 * CM_W]
    gt_ref[0, 0] = p[:, OFF_GT:OFF_GT + LANES].T[0:N_GATES, :]


def _in_proj(stream, tt, mod_l, g1, w_in, layer, b_in, gqk, gones, cos_t, sa_t, sb_t, n_ctx_tiles):
    x_arrays, x_specs = _stream_specs(stream, n_ctx_tiles)
    bsz, d = x_arrays[0].shape[0], D_MODEL
    tm = ROW_TILE
    nt = tt // tm
    row = lambda b, t: (b, t, 0)
    const = lambda b, t: (0, 0)
    tab = lambda b, t: (t, 0)
    outs = [(QK_W, BF16), (KV_W, BF16), (3 * ML_W, BF16), (ML_W, F32), (2 * CM_W, F32), (LANES, F32)]
    return pl.pallas_call(
        functools.partial(_in_kernel, n_stream=len(x_arrays), n_ctx_tiles=n_ctx_tiles),
        grid=(bsz, nt),
        in_specs=x_specs + [
                  pl.BlockSpec((1, 1, 6 * d), lambda b, t: (jnp.where(t < n_ctx_tiles, bsz, b), 0, 0)),
                  pl.BlockSpec((1, d), const),
                  _layer_spec(w_in, layer),
                  pl.BlockSpec((1, N_PROJ), const),
                  pl.BlockSpec((1, QK_W), const),
                  pl.BlockSpec((LANES, LANES), const),
                  pl.BlockSpec((tm, LANES), tab),
                  pl.BlockSpec((tm, LANES), tab),
                  pl.BlockSpec((tm, LANES), tab)],
        out_specs=[pl.BlockSpec((1, KV_W, tm), lambda b, t: (b, 0, t)) if i == 1
                   else pl.BlockSpec((1, 1, N_GATES, tm), lambda b, t: (b, t, 0, 0)) if i == 5
                   else pl.BlockSpec((1, tm, w), row) for i, (w, _) in enumerate(outs)],
        out_shape=[jax.ShapeDtypeStruct((bsz, KV_W, tt) if i == 1 else (bsz, nt, N_GATES, tm) if i == 5
                                        else (bsz, tt, w), dt) for i, (w, dt) in enumerate(outs)],
        compiler_params=_cparams("parallel", "parallel"),
        name="in_proj",
    )(*x_arrays, mod_l, g1, w_in, b_in, gqk, gones, cos_t, sa_t, sb_t)


def _attn_kernel(q_ref, k_ref, vt_ref, o_ref, s_ref, e_ref, *, n_ctx_tiles, t_ctx):
    t = pl.program_id(1)
    tq = q_ref.shape[1]
    low = lax.broadcasted_iota(jnp.int32, (tq, LANES), 1) < HEAD_DIM
    n_pairs = ATTN_W // LANES
    groups = [(kvh, pp) for pp in range(n_pairs // 2) for kvh in range(2)]

    def attend(kv_len):
        def scores(g):
            k = k_ref[0, 0:kv_len, :]
            kvh, pp = groups[g]
            sel = low if kvh == 0 else jnp.logical_not(low)
            q2 = jnp.concatenate(
                [jnp.where(sel, q_ref[0, :, p * LANES:(p + 1) * LANES], jnp.zeros((tq, LANES), BF16))
                 for p in (2 * pp, 2 * pp + 1)], axis=0)
            s_ref[g, 0:kv_len, :] = lax.dot_general(k, q2, (((1,), (1,)), ((), ())),
                                                        preferred_element_type=F32)

        for g0 in range(len(groups)):
            scores(g0)
        outs = {}
        for g, (kvh, pp) in enumerate(groups):
            chunks = [slice(c, c + ATTN_CHUNK) for c in range(0, kv_len, ATTN_CHUNK)]
            fold = lambda a: a.reshape(ATTN_CHUNK // 8, 8, 2 * tq)
            m8 = None
            for rows in chunks:
                mc = jnp.max(fold(s_ref[g, rows, :]), axis=0)
                m8 = mc if m8 is None else jnp.maximum(m8, mc)
            m = jnp.max(m8, axis=0, keepdims=True)
            for rows in chunks:
                e_ref[g % 2, rows, :] = jnp.exp2(s_ref[g, rows, :] - m).astype(BF16)
            vt = vt_ref[0, kvh * HEAD_DIM:(kvh + 1) * HEAD_DIM, 0:kv_len]
            vt1 = jnp.concatenate([vt, jnp.ones((16, kv_len), BF16)], axis=0)
            ov = jnp.dot(vt1, e_ref[g % 2, 0:kv_len, :], preferred_element_type=F32)
            outs[kvh] = ov[0:HEAD_DIM] / ov[HEAD_DIM:HEAD_DIM + 1]
            if kvh == 1:
                for j, p in enumerate((2 * pp, 2 * pp + 1)):
                    both = jnp.concatenate([outs[0][:, j * tq:(j + 1) * tq], outs[1][:, j * tq:(j + 1) * tq]], axis=0)
                    o_ref[0, :, p * LANES:(p + 1) * LANES] = both.T.astype(BF16)

    @pl.when(t < n_ctx_tiles)
    def _():
        attend(t_ctx)

    @pl.when(t >= n_ctx_tiles)
    def _():
        attend(k_ref.shape[1])


def _attention(qk, vt, t_ctx):
    bsz, tt, _ = qk.shape
    tq = ROW_TILE
    kern = functools.partial(_attn_kernel, n_ctx_tiles=t_ctx // tq, t_ctx=t_ctx)
    return pl.pallas_call(
        kern,
        grid=(bsz, tt // tq),
        in_specs=[pl.BlockSpec((1, tq, ATTN_W), lambda b, t: (b, t, 0)),
                  pl.BlockSpec((1, tt, KV_W), lambda b, t: (b, 0, ATTN_W // KV_W)),
                  pl.BlockSpec((1, KV_W, tt), lambda b, t: (b, 0, 0))],
        out_specs=pl.BlockSpec((1, tq, ATTN_W), lambda b, t: (b, t, 0)),
        out_shape=jax.ShapeDtypeStruct((bsz, tt, ATTN_W), BF16),
        scratch_shapes=[pltpu.VMEM((4, tt, 2 * tq), F32), pltpu.VMEM((2, tt, 2 * tq), BF16)],
        compiler_params=_cparams("parallel", "parallel"),
        name="attention",
    )(qk, qk, vt)


def _log_sigmoid(x):
    return jnp.minimum(x, 0.0) - jnp.log1p(jnp.exp(-jnp.abs(x)))


def _split3(a):
    hi = a.astype(BF16)
    r1 = a - hi.astype(F32)
    mid = r1.astype(BF16)
    lo = (r1 - mid.astype(F32)).astype(BF16)
    return hi, mid, lo


def _mlstm_kernel(qkv_ref, gt_ref, mo_ref, gmh_ref, gones_ref, esel_ref, tril_ref, triu_ref,
                  o_ref, hf_ref, hb_ref, cn_ref, m_ref, neg_ref, cmask_ref, *, n_ctx_chunks):
    L = ML_CHUNK
    W = ML_W
    n_chunks = qkv_ref.shape[1] // L

    row = lax.broadcasted_iota(jnp.int32, (L, L), 0)
    col = lax.broadcasted_iota(jnp.int32, (L, L), 1)
    neg_ref[0] = jnp.where(col <= row, 0.0, -jnp.inf)
    neg_ref[1] = jnp.where(col >= row, 0.0, -jnp.inf)
    blk_r = lax.broadcasted_iota(jnp.int32, (W, 2 * W), 0) // HEAD_DIM
    blk_c = (lax.broadcasted_iota(jnp.int32, (W, 2 * W), 1) % W) // HEAD_DIM
    cmask_ref[...] = (blk_r == blk_c).astype(F32)
    low = lax.broadcasted_iota(jnp.int32, (L, LANES), 1) < HEAD_DIM
    ones_blk = jnp.ones((L, W), BF16)
    cn_ref[...] = jnp.zeros_like(cn_ref)
    m_ref[...] = jnp.zeros_like(m_ref)

    def step(rev, c):
        r0 = pl.multiple_of(c * L, L)
        q = qkv_ref[0, pl.ds(r0, L), 0:W]
        k = qkv_ref[0, pl.ds(r0, L), W:2 * W]
        v = qkv_ref[0, pl.ds(r0, L), 2 * W:3 * W]
        gt = gt_ref[0, c]
        goff = 2 * ML_HEADS if rev else 0
        tri_r = tril_ref[...] if rev else triu_ref[...]
        last = 0 if rev else L - 1

        brows = sum(jnp.dot(part, tri_r, preferred_element_type=F32) for part in _split3(_log_sigmoid(gt)))
        bexp = sum(lax.dot_general(part, esel_ref[rev], (((0,), (0,)), ((), ())), preferred_element_type=F32)
                   for part in _split3(brows))
        r_rows = gt[goff:goff + ML_HEADS] - brows[goff + ML_HEADS:goff + 2 * ML_HEADS]

        cn = cn_ref[rev]
        m_prev = m_ref[rev][0:1, :]
        qc = jnp.dot(q, cn.astype(BF16), preferred_element_type=F32)
        kt = k.astype(F32).T
        kt_b = kt.astype(BF16)

        mus, asums, avs, gsr = [], [], [], []
        for h in range(ML_HEADS):
            cols = slice((h // 2) * LANES, (h // 2 + 1) * LANES)
            sel = low if h % 2 == 0 else jnp.logical_not(low)
            qh = jnp.where(sel, q[:, cols], jnp.zeros((L, LANES), BF16))
            s = jnp.dot(qh, kt_b[cols, :], preferred_element_type=F32)
            r = r_rows[h:h + 1, :]
            rm = r + neg_ref[rev]
            mu = jnp.maximum(jnp.max(rm, axis=-1, keepdims=True), m_prev[:, h * HEAD_DIM:h * HEAD_DIM + 1])
            a = jnp.exp(rm - mu) * s
            asums.append(jnp.sum(a, axis=-1, keepdims=True))
            avs.append(jnp.dot(a.astype(BF16), v[:, cols], preferred_element_type=F32))
            mus.append(mu)
            gsr.append(jnp.broadcast_to(jnp.exp(r - mu[last:last + 1, :]), (HEAD_DIM, L)))

        pair = lambda xs: jnp.concatenate([jnp.where(low, xs[0], xs[1]), jnp.where(low, xs[2], xs[3])], axis=1)
        mu_x, asum_x, av_x = pair(mus), pair(asums), pair(avs)
        w_inter = jnp.exp(m_prev - mu_x)
        m_t = bexp + mu_x
        num = w_inter * qc[:, 0:W] + av_x
        den = w_inter * qc[:, W:2 * W] + asum_x
        dst = hb_ref if rev else hf_ref
        dst[pl.ds(r0, L), :] = num / jnp.maximum(jnp.abs(den), jnp.exp(-m_t))

        mu_last = mu_x[last:last + 1, :]
        wc = jnp.exp(m_prev - mu_last)
        gkt = (kt * jnp.concatenate(gsr, axis=0)).astype(BF16)
        upd = jnp.dot(gkt, jnp.concatenate([v, ones_blk], axis=1), preferred_element_type=F32)
        cn_ref[rev] = jnp.concatenate([wc, wc], axis=1) * cn + upd * cmask_ref[...]
        m_ref[rev] = jnp.broadcast_to(bexp[last:last + 1, :] + mu_last, (8, W))

    def body(j, carry):
        step(0, j)
        cb = jnp.where(j < n_ctx_chunks, n_ctx_chunks - 1 - j, n_chunks - 1 - (j - n_ctx_chunks))
        step(1, cb)
        return carry

    lax.fori_loop(0, n_chunks, body, 0)

    def merge(c, carry):
        r0 = pl.multiple_of(c * L, L)
        hs = hf_ref[pl.ds(r0, L), :] + hb_ref[pl.ds(r0, L), :]
        gms = _group_mean_sq(hs, gones_ref[...])
        y = (hs * lax.rsqrt(gms + EPS)) * gmh_ref[...]
        o_ref[0, pl.ds(r0, L), :] = (jax.nn.sigmoid(mo_ref[0, pl.ds(r0, L), :]) * y).astype(BF16)
        return carry

    lax.fori_loop(0, n_chunks, merge, 0)


def _mlstm(mqkv, gates_t, mo, g_mh, gones, esel, tril, triu, t_ctx):
    bsz, tt, _ = mqkv.shape
    L = ML_CHUNK
    nc = tt // L
    kern = functools.partial(_mlstm_kernel, n_ctx_chunks=t_ctx // L)
    per_b = lambda b: (b, 0, 0)
    const = lambda b: (0, 0)
    return pl.pallas_call(
        kern,
        grid=(bsz,),
        in_specs=[pl.BlockSpec((1, tt, 3 * ML_W), per_b),
                  pl.BlockSpec((1, nc, N_GATES, L), lambda b: (b, 0, 0, 0)),
                  pl.BlockSpec((1, tt, ML_W), per_b),
                  pl.BlockSpec((1, ML_W), const),
                  pl.BlockSpec((LANES, LANES), const),
                  pl.BlockSpec((2, N_GATES, ML_W), lambda b: (0, 0, 0)),
                  pl.BlockSpec((L, L), const),
                  pl.BlockSpec((L, L), const)],
        out_specs=pl.BlockSpec((1, tt, ML_W), per_b),
        out_shape=jax.ShapeDtypeStruct((bsz, tt, ML_W), BF16),
        scratch_shapes=[pltpu.VMEM((tt, ML_W), F32), pltpu.VMEM((tt, ML_W), F32),
                        pltpu.VMEM((2, ML_W, 2 * ML_W), F32), pltpu.VMEM((2, 8, ML_W), F32),
                        pltpu.VMEM((2, L, L), F32), pltpu.VMEM((ML_W, 2 * ML_W), F32)],
        compiler_params=_cparams("parallel"),
        name="mlstm",
    )(mqkv, gates_t, mo, g_mh, gones, esel, tril, triu)


def _chunk_mlp(uv, g_v, g_ones, wsp_ref, b_sp):
    lane = lax.broadcasted_iota(jnp.int32, (CM_CHUNK, CM_W), 1)
    u = jax.nn.gelu(uv[:, 0:CM_W])
    v = jax.nn.gelu(uv[:, CM_W:2 * CM_W])
    gms = _group_mean_sq(v, g_ones)
    vb = ((v * lax.rsqrt(gms + EPS)) * g_v).astype(BF16)
    z = jnp.zeros((CM_CHUNK, CM_W), F32)
    for g in range(CM_GROUPS):
        zg = jnp.dot(wsp_ref[g], vb, preferred_element_type=F32)
        z = jnp.where((lane // HEAD_DIM) == g, zg, z)
    return (u * (z + b_sp)).astype(BF16)


def _out_kernel(*refs, n_stream, n_ctx_tiles, skip_tiles):
    (at_ref, ml_ref, uv_ref, mod_ref, g2_ref, w_ref, gv_ref, gones_ref, wsp_ref, bsp_ref,
     xo_ref, h2_ref) = refs[n_stream:]
    d = D_MODEL
    tm = at_ref.shape[1]
    mod = mod_ref[0]
    g1, sh2, s2 = mod[:, 2 * d:3 * d], mod[:, 3 * d:4 * d], mod[:, 4 * d:5 * d]
    cm = jnp.concatenate([_chunk_mlp(uv_ref[0, c:c + CM_CHUNK, :], gv_ref[...], gones_ref[...], wsp_ref, bsp_ref[...])
                          for c in range(0, tm, CM_CHUNK)], axis=0)
    mm = (jnp.dot(at_ref[0], w_ref[0, 0:ATTN_W, :], preferred_element_type=F32)
          + jnp.dot(ml_ref[0], w_ref[0, ATTN_W:ATTN_W + ML_W, :], preferred_element_type=F32)
          + jnp.dot(cm, w_ref[0, ATTN_W + ML_W:, :], preferred_element_type=F32))
    y = _stream_tile(refs[:n_stream], n_ctx_tiles, skip_tiles) + g1 * mm
    xo_ref[0] = y
    ms = jnp.mean(y * y, axis=-1, keepdims=True)
    h2_ref[0] = ((y * lax.rsqrt(ms + EPS)) * g2_ref[...] * (1 + s2) + sh2).astype(BF16)


def _out_proj(attn, ml, cuv, stream, mod_l, g2, w_out, layer, g_v, gones, w_sp, b_sp_lanes, n_ctx_tiles, skip_tiles):
    x_arrays, x_specs = _stream_specs(stream, n_ctx_tiles, skip_tiles)
    bsz, tt, d = attn.shape[0], attn.shape[1], D_MODEL
    tm = ROW_TILE
    t_out = tt - skip_tiles * tm
    row = lambda b, t: (b, t + skip_tiles, 0)
    out_row = lambda b, t: (b, t, 0)
    const = lambda b, t: (0, 0)
    return pl.pallas_call(
        functools.partial(_out_kernel, n_stream=len(x_arrays), n_ctx_tiles=n_ctx_tiles, skip_tiles=skip_tiles),
        grid=(bsz, t_out // tm),
        in_specs=x_specs + [
                  pl.BlockSpec((1, tm, ATTN_W), row),
                  pl.BlockSpec((1, tm, ML_W), row),
                  pl.BlockSpec((1, tm, 2 * CM_W), row),
                  pl.BlockSpec((1, 1, 6 * d), lambda b, t: (jnp.where(t + skip_tiles < n_ctx_tiles, bsz, b), 0, 0)),
                  pl.BlockSpec((1, d), const),
                  _layer_spec(w_out, layer),
                  pl.BlockSpec((1, CM_W), const),
                  pl.BlockSpec((LANES, LANES), const),
                  pl.BlockSpec((CM_GROUPS, CM_CHUNK, CM_CHUNK), lambda b, t: (0, 0, 0)),
                  pl.BlockSpec((CM_CHUNK, CM_W), const)],
        out_specs=[pl.BlockSpec((1, tm, d), out_row), pl.BlockSpec((1, tm, d), out_row)],
        out_shape=[jax.ShapeDtypeStruct((bsz, t_out, d), F32), jax.ShapeDtypeStruct((bsz, t_out, d), BF16)],
        compiler_params=_cparams("parallel", "parallel"),
        name="out_proj",
    )(*x_arrays, attn, ml, cuv, mod_l, g2, w_out, g_v, gones, w_sp, b_sp_lanes)


def _ffn_kernel(h_ref, hp_ref, hn_ref, x_ref, mod_ref, wu_ref, cw_ref, wd_ref, o_ref, hcat_ref, act_ref, *,
                n_ctx_tiles):
    d = D_MODEL
    tm = h_ref.shape[1]
    t = pl.program_id(1)
    nt = pl.num_programs(1)
    has_prev = jnp.logical_and(t != 0, t != n_ctx_tiles)
    has_next = jnp.logical_and(t != n_ctx_tiles - 1, t != nt - 1)
    hrow = lax.broadcasted_iota(jnp.int32, (HALO, d), 0)
    before = pltpu.roll(hp_ref[0].astype(F32), 1, axis=0)
    after = pltpu.roll(hn_ref[0].astype(F32), 7, axis=0)
    halo = jnp.where(jnp.logical_and(hrow == 0, has_prev), before,
                     jnp.where(jnp.logical_and(hrow == 7, has_next), after, 0.0))
    hcat_ref[0:tm, :] = h_ref[0]
    hcat_ref[tm:tm + HALO, :] = halo.astype(BF16)
    hc = hcat_ref[...]
    sub = lax.broadcasted_iota(jnp.int32, (8, FF_TILE), 0)

    for f in range(D_FF // FF_TILE):
        halves = []
        for off in (f * FF_TILE, D_FF + f * FF_TILE):
            sl = slice(off, off + FF_TILE)
            a = jnp.dot(hc, wu_ref[0, :, sl], preferred_element_type=F32)
            cur = a[0:tm]
            edge = a[tm:tm + 8]
            dn = pltpu.roll(cur, 1, axis=0)
            up = pltpu.roll(cur, tm - 1, axis=0)
            prev = jnp.concatenate([jnp.where(sub == 0, edge, dn[0:8]), dn[8:]], axis=0)
            nxt = jnp.concatenate([up[:tm - 8], jnp.where(sub == 7, edge, up[tm - 8:])], axis=0)
            halves.append(prev * cw_ref[0:1, sl] + cur * cw_ref[1:2, sl] + nxt * cw_ref[2:3, sl] + cw_ref[3:4, sl])
        gate, val = halves
        act_ref[:, f * FF_TILE:(f + 1) * FF_TILE] = ((gate * jax.nn.sigmoid(gate)) * val).astype(BF16)
    acc = jnp.dot(act_ref[...], wd_ref[0], preferred_element_type=F32)
    g2 = mod_ref[0][:, 5 * d:6 * d]
    o_ref[0] = x_ref[0] + g2 * acc


def _conv_ffn(h2, x_mid, mod_l, w_up, cw, w_down, layer, n_ctx_tiles):
    bsz, tt, d = x_mid.shape
    tm = ROW_TILE
    nt = tt // tm
    hb = tm // HALO
    n_halo_blocks = tt // HALO
    row = lambda b, t: (b, t, 0)
    const = lambda b, t: (0, 0)
    kern = functools.partial(_ffn_kernel, n_ctx_tiles=n_ctx_tiles)
    return pl.pallas_call(
        kern,
        grid=(bsz, nt),
        in_specs=[pl.BlockSpec((1, tm, d), row),
                  pl.BlockSpec((1, HALO, d), lambda b, t: (b, jnp.maximum(t * hb - 1, 0), 0)),
                  pl.BlockSpec((1, HALO, d), lambda b, t: (b, jnp.minimum((t + 1) * hb, n_halo_blocks - 1), 0)),
                  pl.BlockSpec((1, tm, d), row),
                  pl.BlockSpec((1, 1, 6 * d), lambda b, t: (jnp.where(t < n_ctx_tiles, bsz, b), 0, 0)),
                  _layer_spec(w_up, layer),
                  pl.BlockSpec((8, 2 * D_FF), const),
                  _layer_spec(w_down, layer)],
        out_specs=pl.BlockSpec((1, tm, d), row),
        out_shape=jax.ShapeDtypeStruct((bsz, tt, d), F32),
        scratch_shapes=[pltpu.VMEM((tm + HALO, d), BF16), pltpu.VMEM((tm, D_FF), BF16)],
        compiler_params=_cparams("parallel", "parallel"),
        name="conv_ffn",
    )(h2, h2, h2, x_mid, mod_l, w_up, cw, w_down)


def _rope_tables(s_len, t_ctx):
    nf = HEAD_DIM // 4
    rows = s_len // GRID_W
    pos = jnp.arange(s_len)
    row = jnp.repeat(jnp.arange(rows), GRID_W).astype(F32)
    colp = (pos % GRID_W).astype(F32)
    inv = ROPE_THETA ** (-jnp.arange(nf, dtype=F32) / nf)
    ang_r, ang_c = row[:, None] * inv[None], colp[:, None] * inv[None]
    cr, cc, sr, sc = jnp.cos(ang_r), jnp.cos(ang_c), jnp.sin(ang_r), jnp.sin(ang_c)
    z = jnp.zeros_like(sr)
    cos = jnp.concatenate([cr, cr, cc, cc], axis=1)
    sa = jnp.concatenate([-sr, z, -sc, z], axis=1)
    sb = jnp.concatenate([z, sr, z, sc], axis=1)
    pad = lambda a, fill: jnp.concatenate([jnp.full((t_ctx, HEAD_DIM), fill, F32), a], axis=0)
    two = lambda a: jnp.concatenate([a, a], axis=1)
    return two(pad(cos, 1.0)), two(pad(sa, 0.0)), two(pad(sb, 0.0))


def kernel(x, c, ctx, c_ctx, w_ada, b_ada, g_norm1, w_in, b_in, g_q, g_k, g_mh, g_v, w_sp, b_sp,
           w_out, g_norm2, w_up, conv_w, conv_b, w_down):
    bsz, s_len, d = x.shape
    t_ctx = ctx.shape[1]
    depth = w_ada.shape[0]
    tt = t_ctx + s_len
    assert d == D_MODEL and t_ctx % ML_CHUNK == 0 and s_len % ML_CHUNK == 0 and ROW_TILE == ML_CHUNK
    n_ctx_tiles = t_ctx // ROW_TILE

    o = ATTN_W + 2 * KV_W
    gate_off = o + 4 * ML_W
    cm_off = gate_off + N_GATES
    segs = ([(h * HEAD_DIM, (h + 1) * HEAD_DIM) for h in Q_HEAD_ORDER]
            + [(ATTN_W, gate_off), (cm_off, cm_off + 2 * CM_W), (gate_off, gate_off + N_GATES)])
    pad = N_PROJ - sum(e - s for s, e in segs)
    w_in_p = jnp.concatenate([w_in[:, :, s:e].astype(BF16) for s, e in segs]
                             + [jnp.zeros((depth, d, pad), BF16)], axis=2)
    b_in_p = jnp.concatenate([b_in[:, s:e] for s, e in segs] + [jnp.zeros((depth, pad), F32)], axis=1)[:, None, :]
    gqk = jnp.concatenate([jnp.tile(g_q, (1, N_Q_HEADS)) * (HEAD_DIM ** -0.5 * LOG2_E), jnp.tile(g_k, (1, N_KV_HEADS))],
                          axis=1)[:, None, :]
    w_out_p = jnp.concatenate([w_out[:, h * HEAD_DIM:(h + 1) * HEAD_DIM, :].astype(BF16) for h in Q_HEAD_ORDER]
                              + [w_out[:, ATTN_W:, :].astype(BF16)], axis=1)
    w_up_b = w_up.astype(BF16)
    w_down_b = w_down.astype(BF16)
    cw = jnp.concatenate([conv_w, conv_b[:, None, :], jnp.zeros((depth, 4, 2 * D_FF), F32)], axis=1)
    w_sp_b = w_sp.astype(BF16)
    b_sp_l = jnp.repeat(jnp.swapaxes(b_sp, 1, 2), HEAD_DIM, axis=2)
    blk = np.arange(LANES) // HEAD_DIM
    gones = jnp.asarray(blk[:, None] == blk[None, :], BF16)
    ii = np.arange(ML_CHUNK)
    tril = jnp.asarray(ii[None, :] <= ii[:, None], BF16)
    triu = jnp.asarray(ii[None, :] >= ii[:, None], BF16)
    gate_row = np.arange(N_GATES)[:, None]
    lane_head = (np.arange(ML_W) // HEAD_DIM)[None, :]
    esel = jnp.asarray(np.stack([gate_row == ML_HEADS + lane_head, gate_row == 3 * ML_HEADS + lane_head]), BF16)
    cos_t, sa_t, sb_t = _rope_tables(s_len, t_ctx)

    n_rows = -(-(bsz + 1) // 8) * 8
    cs = jnp.concatenate([c, c_ctx[None, :], jnp.zeros((n_rows - bsz - 1, d), F32)], axis=0)
    mod = _modulation(cs, w_ada, b_ada)

    x_all = (ctx, x)
    for l in range(depth):
        mod_l = mod[l][:, None, :]
        qk, v, mqkv, mo, cuv, gates_t = _in_proj(x_all, tt, mod_l, g_norm1[l][None], w_in_p, l, b_in_p[l], gqk[l],
                                                 gones, cos_t, sa_t, sb_t, n_ctx_tiles)
        attn = _attention(qk, v, t_ctx)
        ml = _mlstm(mqkv, gates_t, mo, g_mh[l][None], gones, esel, tril, triu, t_ctx)
        skip = n_ctx_tiles if l == depth - 1 else 0
        x_mid, h2 = _out_proj(attn, ml, cuv, x_all, mod_l, g_norm2[l][None], w_out_p, l, g_v[l][None], gones,
                              w_sp_b[l], b_sp_l[l], n_ctx_tiles, skip)
        x_all = _conv_ffn(h2, x_mid, mod_l, w_up_b, cw[l], w_down_b, l, n_ctx_tiles - skip)
    return x_all
```
